```python
import jax, jax.numpy as jnp
from jax import lax
import numpy as np

D_MODEL = 1024
BATCH = 8
SEQ = 4096
DEPTH = 1

GRID_W = 64
CTX_LEN = 256
D_SSD = 1024
SSD_HEAD_DIM = 64
SSD_HEADS = D_SSD // SSD_HEAD_DIM
SSD_GROUPS = 2
SSD_HPG = SSD_HEADS // SSD_GROUPS
SSD_STATE = 128
SSD_CONV = 5
SSD_CHUNK = 128
SSD_GN = SSD_GROUPS * SSD_STATE
XBC_DIM = D_SSD + 2 * SSD_GN
D_GM = 1024
GM_GROUPS = 8
GM_GROUP_DIM = D_GM // GM_GROUPS
GM_CHUNK = 128
ROWS_PER_CHUNK = GM_CHUNK // GRID_W
D_FF = -(-8 * D_MODEL // (3 * 256)) * 256
D_PROJ = D_SSD + XBC_DIM + 2 * SSD_HEADS + 2 * D_GM + 2 * D_MODEL
ALPHA = (2 * DEPTH) ** 0.25
BETA = (8 * DEPTH) ** -0.25
LN_EPS = 1e-5

kernel_name = 'hybrid_ssd_gmlp_dit_block'


def layer_norm(x, g, b):
    xf = x.astype(jnp.float32)
    mu = jnp.mean(xf, axis=-1, keepdims=True)
    var = jnp.mean(jnp.square(xf - mu), axis=-1, keepdims=True)
    return ((xf - mu) * lax.rsqrt(var + LN_EPS) * g + b).astype(x.dtype)


def gated_rms_norm(y, z, g):
    h = (y * jax.nn.silu(z)).astype(jnp.float32)
    h = h * lax.rsqrt(jnp.mean(jnp.square(h), axis=-1, keepdims=True) + LN_EPS)
    return (h * g).astype(y.dtype)


def dwconv_centred(x, w, b):
    ch = x.shape[-1]
    pad = SSD_CONV // 2
    y = lax.conv_general_dilated(x, w[:, None, :], window_strides=(1,), padding=[(pad, pad)],
                                 dimension_numbers=('NWC', 'WIO', 'NWC'), feature_group_count=ch)
    return y + b


def split_proj(p):
    o1 = D_SSD
    o2 = o1 + XBC_DIM
    o3 = o2 + 2 * SSD_HEADS
    o4 = o3 + D_GM
    o5 = o4 + D_GM
    return p[..., :o1], p[..., o1:o2], p[..., o2:o3], p[..., o3:o4], p[..., o4:o5], p[..., o5:]


def ssd_scan(xs, dt, A, Bm, Cm, h0):
    bsz, L, _ = xs.shape
    nc = L // SSD_CHUNK
    x = xs.reshape(bsz, nc, SSD_CHUNK, SSD_GROUPS, SSD_HPG, SSD_HEAD_DIM)
    dt = dt.reshape(bsz, nc, SSD_CHUNK, SSD_GROUPS, SSD_HPG)
    Bm = Bm.reshape(bsz, nc, SSD_CHUNK, SSD_GROUPS, SSD_STATE)
    Cm = Cm.reshape(bsz, nc, SSD_CHUNK, SSD_GROUPS, SSD_STATE)
    acum = jnp.cumsum(dt * A.reshape(SSD_GROUPS, SSD_HPG), axis=2)
    seg = acum[:, :, :, None] - acum[:, :, None, :]
    tri = jnp.tril(jnp.ones((SSD_CHUNK, SSD_CHUNK), dtype=bool))[:, :, None, None]
    lmat = jnp.where(tri, jnp.exp(jnp.where(tri, seg, 0.0)), 0.0)
    cb = jnp.einsum('bcign,bcjgn->bcijg', Cm, Bm)
    wgt = cb[..., None] * lmat * dt[:, :, None]
    y_diag = jnp.einsum('bcijgh,bcjghp->bcighp', wgt, x)
    decay_to_end = jnp.exp(acum[:, :, -1:] - acum)
    states = jnp.einsum('bcjgn,bcjgh,bcjghp->bcghpn', Bm, decay_to_end * dt, x)
    chunk_decay = jnp.exp(acum[:, :, -1])

    def step(h, inp):
        dec, st = inp
        h_new = (dec[..., None, None] * h + st).astype(h.dtype)
        return h_new, h

    h_final, h_prev = lax.scan(step, h0.astype(states.dtype),
                               (jnp.moveaxis(chunk_decay, 1, 0), jnp.moveaxis(states, 1, 0)))
    h_prev = jnp.moveaxis(h_prev, 0, 1)
    y_off = jnp.einsum('bcign,bcghpn,bcigh->bcighp', Cm, h_prev, jnp.exp(acum))
    return (y_diag + y_off).reshape(bsz, L, D_SSD), h_final


def ssd_branch(z, xbc, dt_raw, conv_w, conv_b, dt_bias, a_log, d_skip, norm_g, h0_f, h0_b):
    bsz, L, _ = xbc.shape
    xbc = jax.nn.silu(dwconv_centred(xbc, conv_w, conv_b))
    xs, Bm, Cm = xbc[..., :D_SSD], xbc[..., D_SSD:D_SSD + SSD_GN], xbc[..., D_SSD + SSD_GN:]
    dt = jax.nn.softplus(dt_raw.reshape(bsz, L, 2, SSD_HEADS) + dt_bias)
    A = -jnp.exp(a_log)
    y_f, s_f = ssd_scan(xs, dt[:, :, 0], A[0], Bm, Cm, h0_f)
    y_b, s_b = ssd_scan(jnp.flip(xs, 1), jnp.flip(dt[:, :, 1], 1), A[1],
                        jnp.flip(Bm, 1), jnp.flip(Cm, 1), h0_b)
    skip = (xs.reshape(bsz, L, SSD_HEADS, SSD_HEAD_DIM) * (d_skip[0] + d_skip[1])[:, None]).reshape(bsz, L, D_SSD)
    y = y_f + jnp.flip(y_b, 1) + skip
    return gated_rms_norm(y, z, norm_g), s_f, s_b


def spatial_gating(u, v, n_chunks, g, b, w_s, b_s):
    bsz = u.shape[0]
    v = layer_norm(v, g, b).reshape(bsz, n_chunks, GM_CHUNK, GM_GROUPS, GM_GROUP_DIM)
    mixed = jnp.einsum('gpq,bnqgc->bnpgc', w_s, v) + b_s.T[:, :, None]
    return u * mixed.reshape(u.shape)


def branch_merge(y_ssd, u, v, gates, n_chunks, gm_g, gm_b, w_s, b_s, b_gate, w_ssd_proj, w_gm_proj, w_out):
    y_gm = spatial_gating(jax.nn.gelu(u), jax.nn.gelu(v), n_chunks, gm_g, gm_b, w_s, b_s)
    g = jax.nn.sigmoid(gates + b_gate)
    merged = g[..., :D_MODEL] * (y_ssd @ w_ssd_proj) + g[..., D_MODEL:] * (y_gm @ w_gm_proj)
    return merged @ w_out


def swiglu(h, w1, w3, w2):
    return (jax.nn.silu(h @ w1) * (h @ w3)) @ w2


def setup_inputs(seed: int = 0) -> dict:
    key = jax.random.key(seed)
    ks = jax.random.split(key, 32)
    f32 = jnp.float32

    def nrm(k, shape, scale=1.0):
        return jax.random.normal(k, shape, f32) * scale

    dt0 = jnp.exp(jax.random.uniform(ks[8], (DEPTH, 2, SSD_HEADS), f32, np.log(1e-3), np.log(1e-1)))
    return {
        'x': nrm(ks[0], (BATCH, SEQ, D_MODEL)),
        'c': nrm(ks[1], (BATCH, D_MODEL)),
        'ctx': nrm(ks[2], (BATCH, CTX_LEN, D_MODEL)),
        'c_ctx': nrm(ks[3], (D_MODEL,)),
        'ln0_g': 1.0 + nrm(ks[4], (D_MODEL,), 0.01),
        'ln0_b': nrm(ks[5], (D_MODEL,), 0.01),
        'w_ada': nrm(ks[6], (DEPTH, D_MODEL, 6 * D_MODEL), 0.3 * D_MODEL ** -0.5),
        'b_ada': nrm(ks[7], (DEPTH, 6 * D_MODEL), 0.01),
        'w_in': nrm(ks[9], (DEPTH, D_MODEL, D_PROJ), D_MODEL ** -0.5),
        'conv_w': nrm(ks[10], (DEPTH, SSD_CONV, XBC_DIM), SSD_CONV ** -0.5),
        'conv_b': nrm(ks[11], (DEPTH, XBC_DIM), 0.01),
        'dt_bias': dt0 + jnp.log(-jnp.expm1(-dt0)),
        'a_log': jnp.log(jax.random.uniform(ks[12], (DEPTH, 2, SSD_HEADS), f32, 1.0, 16.0)),
        'd_skip': 1.0 + nrm(ks[13], (DEPTH, 2, SSD_HEADS), 0.01),
        'ssd_norm_g': 1.0 + nrm(ks[14], (DEPTH, D_SSD), 0.01),
        'gm_norm_g': 1.0 + nrm(ks[15], (DEPTH, D_GM), 0.01),
        'gm_norm_b': nrm(ks[16], (DEPTH, D_GM), 0.01),
        'w_spatial': nrm(ks[17], (DEPTH, GM_GROUPS, GM_CHUNK, GM_CHUNK), GM_CHUNK ** -0.5),
        'b_spatial': 1.0 + nrm(ks[18], (DEPTH, GM_GROUPS, GM_CHUNK), 0.01),
        'b_gate': nrm(ks[19], (DEPTH, 2 * D_MODEL), 0.01),
        'w_ssd_proj': nrm(ks[20], (DEPTH, D_SSD, D_MODEL), BETA * D_SSD ** -0.5),
        'w_gm_proj': nrm(ks[21], (DEPTH, D_GM, D_MODEL), BETA * D_GM ** -0.5),
        'w_out': nrm(ks[22], (DEPTH, D_MODEL, D_MODEL), BETA * D_MODEL ** -0.5),
        'ln1_g': 1.0 + nrm(ks[23], (DEPTH, D_MODEL), 0.01),
        'ln1_b': nrm(ks[24], (DEPTH, D_MODEL), 0.01),
        'w_ff1': nrm(ks[25], (DEPTH, D_MODEL, D_FF), BETA * D_MODEL ** -0.5),
        'w_ff3': nrm(ks[26], (DEPTH, D_MODEL, D_FF), BETA * D_MODEL ** -0.5),
        'w_ff2': nrm(ks[27], (DEPTH, D_FF, D_MODEL), BETA * D_FF ** -0.5),
        'ln2_g': 1.0 + nrm(ks[28], (DEPTH, D_MODEL), 0.01),
        'ln2_b': nrm(ks[29], (DEPTH, D_MODEL), 0.01),
    }


def reference(x, c, ctx, c_ctx, ln0_g, ln0_b, w_ada, b_ada, w_in, conv_w, conv_b, dt_bias, a_log, d_skip,
              ssd_norm_g, gm_norm_g, gm_norm_b, w_spatial, b_spatial, b_gate, w_ssd_proj, w_gm_proj, w_out,
              ln1_g, ln1_b, w_ff1, w_ff3, w_ff2, ln2_g, ln2_b):
    x = layer_norm(x, ln0_g, ln0_b)
    ctx_h = layer_norm(ctx, ln0_g, ln0_b)
    rows = x.shape[1] // GRID_W
    n_lat_chunks = rows // ROWS_PER_CHUNK
    n_ctx_chunks = ctx.shape[1] // GM_CHUNK
    zero_state = jnp.zeros((ctx.shape[0], SSD_GROUPS, SSD_HPG, SSD_HEAD_DIM, SSD_STATE), x.dtype)
    for l in range(DEPTH):
        mod_x = (jax.nn.silu(c) @ w_ada[l] + b_ada[l])[:, None, :]
        mod_c = jax.nn.silu(c_ctx) @ w_ada[l] + b_ada[l]
        sh1x, sc1x, g1x, sh2x, sc2x, g2x = jnp.split(mod_x, 6, axis=-1)
        sh1c, sc1c, g1c, sh2c, sc2c, g2c = jnp.split(mod_c, 6, axis=-1)
        ssd_p = (conv_w[l], conv_b[l], dt_bias[l], a_log[l], d_skip[l], ssd_norm_g[l])
        mrg_p = (gm_norm_g[l], gm_norm_b[l], w_spatial[l], b_spatial[l], b_gate[l],
                 w_ssd_proj[l], w_gm_proj[l], w_out[l])
        zc, xbcc, dtc, uc, vc, gc = split_proj((ctx_h * (1.0 + sc1c) + sh1c) @ w_in[l])
        yc, s_f, s_b = ssd_branch(zc, xbcc, dtc, *ssd_p, zero_state, zero_state)
        zx, xbcx, dtx, ux, vx, gx = split_proj((x * (1.0 + sc1x) + sh1x) @ w_in[l])
        yx, _, _ = ssd_branch(zx, xbcx, dtx, *ssd_p, s_f, s_b)
        out_x = branch_merge(yx, ux, vx, gx, n_lat_chunks, *mrg_p)
        x = layer_norm(ALPHA * x + g1x * out_x, ln1_g[l], ln1_b[l])
        x = layer_norm(ALPHA * x + g2x * swiglu(x * (1.0 + sc2x) + sh2x, w_ff1[l], w_ff3[l], w_ff2[l]),
                       ln2_g[l], ln2_b[l])
        if l < DEPTH - 1:
            out_c = branch_merge(yc, uc, vc, gc, n_ctx_chunks, *mrg_p)
            ctx_h = layer_norm(ALPHA * ctx_h + g1c * out_c, ln1_g[l], ln1_b[l])
            ctx_h = layer_norm(ALPHA * ctx_h + g2c * swiglu(ctx_h * (1.0 + sc2c) + sh2c, w_ff1[l], w_ff3[l], w_ff2[l]),
                               ln2_g[l], ln2_b[l])
    return x
```

```python
import functools

import jax
import jax.numpy as jnp
from jax import lax
from jax.experimental import pallas as pl
from jax.experimental.pallas import tpu as pltpu

f32 = jnp.float32
bf16 = jnp.bfloat16

D = 1024
HEADS = 16
HEAD_DIM = 64
GROUPS = 2
GROUP_W = D // GROUPS
STATE = 128
CONV = 5
CHUNK = 128
XBC = D + 2 * GROUPS * STATE
GM_GROUPS = 8
GM_GROUP_DIM = D // GM_GROUPS
D_FF = 2816
D_MOD = 6 * D
DEPTH = 1
ALPHA = (2 * DEPTH) ** 0.25
LN_EPS = 1e-5
HALO = 8
LANES = 128

OFF_Z, OFF_XBC, OFF_U, OFF_V, OFF_G, W_MAIN = 0, 1024, 2560, 3584, 4608, 6656

VMEM_LIMIT = 56 * 1024 * 1024


def _dot(a, b):
    return jnp.dot(a, b, preferred_element_type=f32)


def _dot_nt(a, b):
    return lax.dot_general(a, b, (((1,), (1,)), ((), ())), preferred_element_type=f32)


def _dot_tn(a, b):
    return lax.dot_general(a, b, (((0,), (0,)), ((), ())), preferred_element_type=f32)


def _ln(xf, g, b):
    mu = jnp.mean(xf, axis=-1, keepdims=True)
    xc = xf - mu
    var = jnp.mean(xc * xc, axis=-1, keepdims=True)
    return xc * lax.rsqrt(var + LN_EPS) * g + b


def _silu(x):
    return x * jax.nn.sigmoid(x)


def _softplus(x):
    return jnp.maximum(x, 0.0) + jnp.log1p(jnp.exp(-jnp.abs(x)))


def _scan_rows(a, tri):
    hi = a.astype(bf16)
    r1 = a - hi.astype(f32)
    mid = r1.astype(bf16)
    lo = (r1 - mid.astype(f32)).astype(bf16)
    cs = _dot(jnp.concatenate([hi, mid, lo], axis=0), tri)
    return cs[0:16] + cs[16:32] + cs[32:48]


def _dir_rows(dt, a_col, fwd):
    k = lax.broadcasted_iota(jnp.int32, (CHUNK, CHUNK), 0)
    j = lax.broadcasted_iota(jnp.int32, (CHUNK, CHUNK), 1)
    tri = jnp.where((k <= j) if fwd else (k >= j), 1.0, 0.0).astype(bf16)
    cum = _scan_rows(dt * a_col, tri)
    total = cum[:, CHUNK - 1:CHUNK] if fwd else cum[:, 0:1]
    return cum, jnp.exp(total - cum) * dt, jnp.exp(cum)


def _to_cols(rows):
    pad = jnp.zeros((LANES - 16 * len(rows), CHUNK), f32)
    return jnp.concatenate(list(rows) + [pad], axis=0).T


def _expand_heads(m):
    lane = lax.broadcasted_iota(jnp.int32, (CHUNK, LANES), 1)
    pieces = []
    for p in range(HEADS // 2):
        pieces.append(jnp.where(lane < HEAD_DIM, m[:, 2 * p:2 * p + 1], m[:, 2 * p + 1:2 * p + 2]))
    return jnp.concatenate(pieces, axis=1)


def _ssd_chunk(xs_bf, xs_f, bm, cm, dt, a_col, h_ref, fwd, want_y=True):
    cum, wdt, ecum = _dir_rows(dt, a_col, fwd)
    cols = _to_cols([cum, wdt, ecum])
    ecum_x = _expand_heads(cols[:, 32:48])
    y = None
    if want_y:
        ii = lax.broadcasted_iota(jnp.int32, (CHUNK, CHUNK), 0)
        jj = lax.broadcasted_iota(jnp.int32, (CHUNK, CHUNK), 1)
        keep = (ii >= jj) if fwd else (ii <= jj)
        lane = lax.broadcasted_iota(jnp.int32, (CHUNK, LANES), 1)
        cb = [_dot_nt(cm[:, g * STATE:(g + 1) * STATE], bm[:, g * STATE:(g + 1) * STATE]) for g in range(GROUPS)]
        pieces = []
        for p in range(HEADS // 2):
            g = (2 * p) // (HEADS // GROUPS)
            ws = []
            for h in (2 * p, 2 * p + 1):
                seg = cols[:, h:h + 1] - cum[h:h + 1, :]
                ws.append(jnp.where(keep, cb[g] * jnp.exp(seg) * dt[h:h + 1, :], 0.0))
            lhs = jnp.concatenate(ws, axis=1).astype(bf16)
            x2 = xs_bf[:, p * LANES:(p + 1) * LANES]
            zero = jnp.zeros_like(x2)
            rhs = jnp.concatenate([jnp.where(lane < HEAD_DIM, x2, zero),
                                   jnp.where(lane >= HEAD_DIM, x2, zero)], axis=0)
            pieces.append(_dot(lhs, rhs))
        y_diag = jnp.concatenate(pieces, axis=1)
        y_off = jnp.concatenate([_dot(cm[:, g * STATE:(g + 1) * STATE], h_ref[g].astype(bf16))
                                 for g in range(GROUPS)], axis=1)
        y = y_diag + y_off * ecum_x
    xw = (xs_f * _expand_heads(cols[:, 16:32])).astype(bf16)
    dec = ecum_x[CHUNK - 1:CHUNK, :] if fwd else ecum_x[0:1, :]
    for g in range(GROUPS):
        st = _dot_tn(bm[:, g * STATE:(g + 1) * STATE], xw[:, g * GROUP_W:(g + 1) * GROUP_W])
        h_ref[g] = dec[:, g * GROUP_W:(g + 1) * GROUP_W] * h_ref[g] + st
    return y


def _conv_silu(ext_ref, convw_ref, convb_ref, t, c0, c1):
    acc = jnp.broadcast_to(convb_ref[:, c0:c1], (t, c1 - c0))
    for k in range(CONV):
        acc = acc + ext_ref[pl.ds(HALO - CONV // 2 + k, t), c0:c1] * convw_ref[k:k + 1, c0:c1]
    return _silu(acc)


def _mod_kernel(c_ref, w_ref, b_ref, o_ref):
    s = _silu(c_ref[...]).astype(bf16)
    o_ref[...] = _dot(s, w_ref[...].astype(bf16)) + b_ref[...]


def _mod_call(cc, w_ada, b_ada):
    nb = 1024
    return pl.pallas_call(
        _mod_kernel,
        grid=(D_MOD // nb,),
        in_specs=[pl.BlockSpec((16, D), lambda n: (0, 0)),
                  pl.BlockSpec((D, nb), lambda n: (0, n)),
                  pl.BlockSpec((1, nb), lambda n: (0, n))],
        out_specs=pl.BlockSpec((16, nb), lambda n: (0, n)),
        out_shape=jax.ShapeDtypeStruct((16, D_MOD), f32),
        compiler_params=pltpu.CompilerParams(dimension_semantics=("arbitrary",), vmem_limit_bytes=VMEM_LIMIT),
        name="adaln_mod",
    )(cc, w_ada, b_ada)


def _ctx_kernel(ctx_ref, mod_ref, ln0g_ref, ln0b_ref, wxbc_ref, wdtT_ref, convw_ref, convb_ref, dtb_ref,
                alog_ref, sf_ref, sb_ref, ext_ref, xs_ref, bc_ref):
    cl = ctx_ref.shape[0]
    sh1 = mod_ref[:, 0:D]
    sc1 = mod_ref[:, D:2 * D]
    xm = (_ln(ctx_ref[...], ln0g_ref[...], ln0b_ref[...]) * (1.0 + sc1) + sh1).astype(bf16)
    ext_ref[pl.ds(0, HALO), :] = jnp.zeros((HALO, XBC), f32)
    ext_ref[pl.ds(HALO + cl, HALO), :] = jnp.zeros((HALO, XBC), f32)
    ext_ref[pl.ds(HALO, cl), :] = _dot(xm, wxbc_ref[...])
    for c0 in range(0, D, 512):
        xs_ref[:, c0:c0 + 512] = _conv_silu(ext_ref, convw_ref, convb_ref, cl, c0, c0 + 512)
    bc_ref[...] = _conv_silu(ext_ref, convw_ref, convb_ref, cl, D, XBC).astype(bf16)
    dt = _softplus(_dot_nt(wdtT_ref[...], xm) + dtb_ref[...])
    a_col = -jnp.exp(alog_ref[...])
    sf_ref[...] = jnp.zeros(sf_ref.shape, f32)
    sb_ref[...] = jnp.zeros(sb_ref.shape, f32)
    nch = cl // CHUNK
    for fwd, h_ref, order in ((True, sf_ref, range(nch)), (False, sb_ref, range(nch - 1, -1, -1))):
        r0 = 0 if fwd else HEADS
        for c in order:
            rows = slice(c * CHUNK, (c + 1) * CHUNK)
            xs_f = xs_ref[rows, :]
            _ssd_chunk(None, xs_f, bc_ref[rows, 0:GROUPS * STATE], None, dt[r0:r0 + HEADS, rows],
                       a_col[r0:r0 + HEADS], h_ref, fwd, want_y=False)


def _ctx_call(ctx, mod3, ln0g, ln0b, w_xbc, w_dtT, convw, convb, dtb_col, alog_col):
    bsz, cl, _ = ctx.shape
    full = lambda shape: pl.BlockSpec(shape, lambda b: (0,) * len(shape))
    st_shape = jax.ShapeDtypeStruct((bsz, GROUPS, STATE, GROUP_W), f32)
    st_spec = pl.BlockSpec((None, GROUPS, STATE, GROUP_W), lambda b: (b, 0, 0, 0))
    return pl.pallas_call(
        _ctx_kernel,
        grid=(bsz,),
        in_specs=[pl.BlockSpec((None, cl, D), lambda b: (b, 0, 0)),
                  pl.BlockSpec((None, 1, 2 * D), lambda b: (0, 0, 0)),
                  full((1, D)), full((1, D)), full((D, XBC)), full((2 * HEADS, D)),
                  full((CONV, XBC)), full((1, XBC)), full((2 * HEADS, 1)), full((2 * HEADS, 1))],
        out_specs=[st_spec, st_spec],
        out_shape=[st_shape, st_shape],
        scratch_shapes=[pltpu.VMEM((cl + 2 * HALO, XBC), f32), pltpu.VMEM((cl, D), f32),
                        pltpu.VMEM((cl, 2 * GROUPS * STATE), bf16)],
        compiler_params=pltpu.CompilerParams(dimension_semantics=("arbitrary",), vmem_limit_bytes=VMEM_LIMIT),
        name="ctx_states",
    )(ctx, mod3, ln0g, ln0b, w_xbc, w_dtT, convw, convb, dtb_col, alog_col)


def _pass1_kernel(x_ref, xp_ref, xn_ref, mod_ref, ln0g_ref, ln0b_ref, wmain_ref, wdtT_ref, convw_ref, convb_ref,
                  dtb_ref, alog_ref, sb_ref,
                  z_ref, xs_ref, bc_ref, yb_ref, u_ref, v_ref, gt_ref, dt_ref,
                  ext_ref, hb_ref, xsf_ref):
    t = x_ref.shape[0]
    step = pl.program_id(1)
    nt = pl.num_programs(1)
    tile = nt - 1 - step

    @pl.when(step == 0)
    def _():
        hb_ref[...] = sb_ref[...]

    sh1 = mod_ref[:, 0:D]
    sc1 = mod_ref[:, D:2 * D]
    g0 = ln0g_ref[...]
    b0 = ln0b_ref[...]

    def lnmod(xf):
        return _ln(xf, g0, b0) * (1.0 + sc1) + sh1

    xm_f = lnmod(x_ref[...])
    xm = xm_f.astype(bf16)
    ext = jnp.concatenate([lnmod(xp_ref[...]), xm_f, lnmod(xn_ref[...])], axis=0).astype(bf16)
    xbc_ext = _dot(ext, wmain_ref[:, OFF_XBC:OFF_U])
    row = lax.broadcasted_iota(jnp.int32, (t + 2 * HALO, 1), 0)
    valid = jnp.logical_and(jnp.logical_or(row >= HALO, tile > 0), jnp.logical_or(row < HALO + t, tile < nt - 1))
    ext_ref[...] = jnp.where(valid, xbc_ext, 0.0)

    for c0 in range(0, D, 512):
        blk = _conv_silu(ext_ref, convw_ref, convb_ref, t, c0, c0 + 512)
        xsf_ref[:, c0:c0 + 512] = blk
        xs_ref[:, c0:c0 + 512] = blk.astype(bf16)
    bc_ref[...] = _conv_silu(ext_ref, convw_ref, convb_ref, t, D, XBC).astype(bf16)

    dt = _softplus(_dot_nt(wdtT_ref[...], xm) + dtb_ref[...])
    dt_ref[...] = dt
    a_col = -jnp.exp(alog_ref[...])

    for c in range(t // CHUNK - 1, -1, -1):
        rows = slice(c * CHUNK, (c + 1) * CHUNK)
        y = _ssd_chunk(xs_ref[rows, :], xsf_ref[rows, :], bc_ref[rows, 0:GROUPS * STATE],
                       bc_ref[rows, GROUPS * STATE:], dt[HEADS:, rows], a_col[HEADS:], hb_ref, fwd=False)
        yb_ref[rows, :] = y.astype(bf16)

    z_ref[...] = _dot(xm, wmain_ref[:, OFF_Z:OFF_XBC]).astype(bf16)
    u_ref[...] = _dot(xm, wmain_ref[:, OFF_U:OFF_V]).astype(bf16)
    v_ref[...] = _dot(xm, wmain_ref[:, OFF_V:OFF_G]).astype(bf16)
    for c0 in range(0, 2 * D, D):
        gt_ref[:, c0:c0 + D] = _dot(xm, wmain_ref[:, OFF_G + c0:OFF_G + c0 + D]).astype(bf16)


def _pass1_call(x, mod3, ln0g, ln0b, w_main, w_dtT, convw, convb, dtb_col, alog_col, s_b, t):
    bsz, seq, _ = x.shape
    nt = seq // t
    hb = t // HALO
    nhb = seq // HALO
    const = lambda shape: pl.BlockSpec(shape, lambda b, s: (0,) * len(shape), pipeline_mode=pl.Buffered(1))
    tok = lambda w: pl.BlockSpec((None, t, w), lambda b, s: (b, nt - 1 - s, 0))
    act = lambda w, dt_: jax.ShapeDtypeStruct((bsz, seq, w), dt_)
    return pl.pallas_call(
        _pass1_kernel,
        grid=(bsz, nt),
        in_specs=[tok(D),
                  pl.BlockSpec((None, HALO, D), lambda b, s: (b, jnp.maximum((nt - 1 - s) * hb - 1, 0), 0)),
                  pl.BlockSpec((None, HALO, D), lambda b, s: (b, jnp.minimum((nt - s) * hb, nhb - 1), 0)),
                  pl.BlockSpec((None, 1, 2 * D), lambda b, s: (b, 0, 0)),
                  const((1, D)), const((1, D)), const((D, W_MAIN)), const((2 * HEADS, D)),
                  const((CONV, XBC)), const((1, XBC)), const((2 * HEADS, 1)), const((2 * HEADS, 1)),
                  pl.BlockSpec((None, GROUPS, STATE, GROUP_W), lambda b, s: (b, 0, 0, 0))],
        out_specs=[tok(D), tok(D), tok(2 * GROUPS * STATE), tok(D), tok(D), tok(D), tok(2 * D),
                   pl.BlockSpec((None, 2 * HEADS, t), lambda b, s: (b, 0, nt - 1 - s))],
        out_shape=[act(D, bf16), act(D, bf16), act(2 * GROUPS * STATE, bf16), act(D, bf16), act(D, bf16),
                   act(D, bf16), act(2 * D, bf16), jax.ShapeDtypeStruct((bsz, 2 * HEADS, seq), f32)],
        scratch_shapes=[pltpu.VMEM((t + 2 * HALO, XBC), f32), pltpu.VMEM((GROUPS, STATE, GROUP_W), f32),
                        pltpu.VMEM((t, D), f32)],
        compiler_params=pltpu.CompilerParams(dimension_semantics=("arbitrary", "arbitrary"),
                                             vmem_limit_bytes=VMEM_LIMIT),
        name="pass1_bwd",
    )(x, x, x, mod3, ln0g, ln0b, w_main, w_dtT, convw, convb, dtb_col, alog_col, s_b)


def _pass2_kernel(x_ref, mod_ref, ln0g_ref, ln0b_ref, z_ref, xs_ref, bc_ref, yb_ref, u_ref, v_ref, gt_ref, dt_ref,
                  alog_ref, dsk_ref, ng_ref, gmg_ref, gmb_ref, ws_ref, bsx_ref, bg_ref, wsp_ref, wgp_ref, wo_ref,
                  ln1g_ref, ln1b_ref, sf_ref, o_ref, hf_ref, ys_ref, vn_ref, ygm_ref):
    t = x_ref.shape[0]
    step = pl.program_id(1)

    @pl.when(step == 0)
    def _():
        hf_ref[...] = sf_ref[...]

    a_col = -jnp.exp(alog_ref[...])
    dt = dt_ref[...]
    dsk = dsk_ref[0:1, :] + dsk_ref[1:2, :]
    nch = t // CHUNK
    for c in range(nch):
        rows = slice(c * CHUNK, (c + 1) * CHUNK)
        xs_bf = xs_ref[rows, :]
        xs_f = xs_bf.astype(f32)
        y = _ssd_chunk(xs_bf, xs_f, bc_ref[rows, 0:GROUPS * STATE], bc_ref[rows, GROUPS * STATE:],
                       dt[0:HEADS, rows], a_col[0:HEADS], hf_ref, fwd=True)
        y = y + yb_ref[rows, :].astype(f32) + xs_f * dsk
        hh = y * _silu(z_ref[rows, :].astype(f32))
        hh = hh * lax.rsqrt(jnp.mean(hh * hh, axis=-1, keepdims=True) + LN_EPS) * ng_ref[...]
        ys_ref[rows, :] = hh.astype(bf16)
        vn_ref[rows, :] = _ln(jax.nn.gelu(v_ref[rows, :].astype(f32)), gmg_ref[...], gmb_ref[...]).astype(bf16)

    for g in range(GM_GROUPS):
        cols = slice(g * GM_GROUP_DIM, (g + 1) * GM_GROUP_DIM)
        rhs = jnp.concatenate([vn_ref[c * CHUNK:(c + 1) * CHUNK, cols] for c in range(nch)], axis=1)
        mixed = _dot(ws_ref[g], rhs)
        for c in range(nch):
            rows = slice(c * CHUNK, (c + 1) * CHUNK)
            ug = jax.nn.gelu(u_ref[rows, cols].astype(f32))
            ygm_ref[rows, cols] = (ug * (mixed[:, c * GM_GROUP_DIM:(c + 1) * GM_GROUP_DIM] + bsx_ref[:, cols])).astype(bf16)

    gate = jax.nn.sigmoid(gt_ref[...].astype(f32) + bg_ref[...])
    merged = gate[:, 0:D] * _dot(ys_ref[...], wsp_ref[...]) + gate[:, D:] * _dot(ygm_ref[...], wgp_ref[...])
    out_x = _dot(merged.astype(bf16), wo_ref[...])
    x0 = _ln(x_ref[...], ln0g_ref[...], ln0b_ref[...])
    g1 = mod_ref[:, 2 * D:3 * D]
    o_ref[...] = _ln(ALPHA * x0 + g1 * out_x, ln1g_ref[...], ln1b_ref[...])


def _pass2_call(x, mod3, ln0g, ln0b, z, xs, bc, yb, u, v, gt, dt, alog_col, dsk, ng, gmg, gmb, ws, bsx, bg,
                wsp, wgp, wo, ln1g, ln1b, s_f, t):
    bsz, seq, _ = x.shape
    nt = seq // t
    const = lambda shape: pl.BlockSpec(shape, lambda b, s: (0,) * len(shape), pipeline_mode=pl.Buffered(1))
    tok = lambda w: pl.BlockSpec((None, t, w), lambda b, s: (b, s, 0))
    return pl.pallas_call(
        _pass2_kernel,
        grid=(bsz, nt),
        in_specs=[tok(D), pl.BlockSpec((None, 1, D_MOD), lambda b, s: (b, 0, 0)), const((1, D)), const((1, D)),
                  tok(D), tok(D), tok(2 * GROUPS * STATE), tok(D), tok(D), tok(D), tok(2 * D),
                  pl.BlockSpec((None, 2 * HEADS, t), lambda b, s: (b, 0, s)),
                  const((2 * HEADS, 1)), const((2, D)), const((1, D)), const((1, D)), const((1, D)),
                  const((GM_GROUPS, CHUNK, CHUNK)), const((CHUNK, D)), const((1, 2 * D)),
                  const((D, D)), const((D, D)), const((D, D)), const((1, D)), const((1, D)),
                  pl.BlockSpec((None, GROUPS, STATE, GROUP_W), lambda b, s: (b, 0, 0, 0))],
        out_specs=tok(D),
        out_shape=jax.ShapeDtypeStruct((bsz, seq, D), f32),
        scratch_shapes=[pltpu.VMEM((GROUPS, STATE, GROUP_W), f32), pltpu.VMEM((t, D), bf16),
                        pltpu.VMEM((t, D), bf16), pltpu.VMEM((t, D), bf16)],
        compiler_params=pltpu.CompilerParams(dimension_semantics=("arbitrary", "arbitrary"),
                                             vmem_limit_bytes=VMEM_LIMIT),
        name="pass2_fwd",
    )(x, mod3, ln0g, ln0b, z, xs, bc, yb, u, v, gt, dt, alog_col, dsk, ng, gmg, gmb, ws, bsx, bg,
      wsp, wgp, wo, ln1g, ln1b, s_f)


def _ffn_kernel(x_ref, mod_ref, w1_ref, w3_ref, w2_ref, ln2g_ref, ln2b_ref, o_ref, a_ref):
    x1 = x_ref[...]
    sh2 = mod_ref[:, 3 * D:4 * D]
    sc2 = mod_ref[:, 4 * D:5 * D]
    g2 = mod_ref[:, 5 * D:6 * D]
    hm = (x1 * (1.0 + sc2) + sh2).astype(bf16)
    nb = 256
    for n0 in range(0, D_FF, nb):
        h1 = _dot(hm, w1_ref[:, n0:n0 + nb])
        h3 = _dot(hm, w3_ref[:, n0:n0 + nb])
        a_ref[:, n0:n0 + nb] = (_silu(h1) * h3).astype(bf16)
    ff = _dot(a_ref[...], w2_ref[...])
    o_ref[...] = _ln(ALPHA * x1 + g2 * ff, ln2g_ref[...], ln2b_ref[...])


def _ffn_call(x1, mod3, w1, w3, w2, ln2g, ln2b, t):
    bsz, seq, _ = x1.shape
    const = lambda shape: pl.BlockSpec(shape, lambda b, s: (0,) * len(shape), pipeline_mode=pl.Buffered(1))
    tok = pl.BlockSpec((None, t, D), lambda b, s: (b, s, 0))
    return pl.pallas_call(
        _ffn_kernel,
        grid=(bsz, seq // t),
        in_specs=[tok, pl.BlockSpec((None, 1, D_MOD), lambda b, s: (b, 0, 0)),
                  const((D, D_FF)), const((D, D_FF)), const((D_FF, D)), const((1, D)), const((1, D))],
        out_specs=tok,
        out_shape=jax.ShapeDtypeStruct((bsz, seq, D), f32),
        scratch_shapes=[pltpu.VMEM((t, D_FF), bf16)],
        compiler_params=pltpu.CompilerParams(dimension_semantics=("arbitrary", "arbitrary"),
                                             vmem_limit_bytes=VMEM_LIMIT),
        name="ffn",
    )(x1, mod3, w1, w3, w2, ln2g, ln2b)


def kernel(x, c, ctx, c_ctx, ln0_g, ln0_b, w_ada, b_ada, w_in, conv_w, conv_b, dt_bias, a_log, d_skip, ssd_norm_g, gm_norm_g, gm_norm_b, w_spatial, b_spatial, b_gate, w_ssd_proj, w_gm_proj, w_out, ln1_g, ln1_b, w_ff1, w_ff3, w_ff2, ln2_g, ln2_b):
    bsz, seq, _ = x.shape
    assert x.shape[2] == D and w_in.shape == (DEPTH, D, W_MAIN + 2 * HEADS) and bsz < 16
    assert seq % 512 == 0 and ctx.shape[1] % CHUNK == 0
    row = lambda a: a.reshape(1, -1)

    w_in0 = w_in[0]
    o_dt = D + XBC
    w_main = jnp.concatenate([w_in0[:, :o_dt], w_in0[:, o_dt + 2 * HEADS:]], axis=1).astype(bf16)
    w_xbc = w_in0[:, D:o_dt].astype(bf16)
    w_dtT = w_in0[:, o_dt:o_dt + 2 * HEADS].T.astype(bf16)
    dtb_col = dt_bias[0].reshape(2 * HEADS, 1)
    alog_col = a_log[0].reshape(2 * HEADS, 1)
    dsk = jnp.repeat(d_skip[0], HEAD_DIM, axis=1)
    bsx = jnp.repeat(b_spatial[0].T, GM_GROUP_DIM, axis=1)
    ln0g, ln0b = row(ln0_g), row(ln0_b)

    cc = jnp.concatenate([c, c_ctx[None, :], jnp.zeros((15 - bsz, D), f32)], axis=0)
    mod = _mod_call(cc, w_ada[0], row(b_ada[0]))
    mod3 = mod.reshape(16, 1, D_MOD)
    s_f, s_b = _ctx_call(ctx, mod3[bsz:bsz + 1], ln0g, ln0b, w_xbc, w_dtT,
                         conv_w[0], row(conv_b[0]), dtb_col, alog_col)

    t = 256
    z, xs, bc, yb, u, v, gt, dt = _pass1_call(x, mod3, ln0g, ln0b, w_main, w_dtT, conv_w[0], row(conv_b[0]),
                                              dtb_col, alog_col, s_b, t)
    x1 = _pass2_call(x, mod3, ln0g, ln0b, z, xs, bc, yb, u, v, gt, dt, alog_col, dsk, row(ssd_norm_g[0]),
                     row(gm_norm_g[0]), row(gm_norm_b[0]), w_spatial[0].astype(bf16), bsx, row(b_gate[0]),
                     w_ssd_proj[0].astype(bf16), w_gm_proj[0].astype(bf16), w_out[0].astype(bf16),
                     row(ln1_g[0]), row(ln1_b[0]), s_f, t)
    return _ffn_call(x1, mod3, w_ff1[0].astype(bf16), w_ff3[0].astype(bf16), w_ff2[0].astype(bf16),
                     row(ln2_g[0]), row(ln2_b[0]), 512)
```

```python
import jax
import jax.numpy as jnp
from jax import lax
from jax.experimental import pallas as pl
from jax.experimental.pallas import tpu as pltpu

f32 = jnp.float32
bf16 = jnp.bfloat16

D = 1024
HEADS = 16
HEAD_DIM = 64
GROUPS = 2
GROUP_W = D // GROUPS
STATE = 128
CONV = 5
CHUNK = 128
XBC = D + 2 * GROUPS * STATE
GM_GROUPS = 8
GM_GROUP_DIM = D // GM_GROUPS
D_FF = 2816
D_MOD = 6 * D
DEPTH = 1
ALPHA = (2 * DEPTH) ** 0.25
LN_EPS = 1e-5
HALO = 8
LANES = 128
NBLK = 256

OFF_Z, OFF_XBC, OFF_U, OFF_V, OFF_G, W_MAIN = 0, 1024, 2560, 3584, 4608, 6656

VMEM_LIMIT = 56 * 1024 * 1024


def _dot(a, b):
    return jnp.dot(a, b, preferred_element_type=f32)


def _dot_nt(a, b):
    return lax.dot_general(a, b, (((1,), (1,)), ((), ())), preferred_element_type=f32)


def _dot_tn(a, b):
    return lax.dot_general(a, b, (((0,), (0,)), ((), ())), preferred_element_type=f32)


def _norm_rows(xf):
    mu = jnp.mean(xf, axis=-1, keepdims=True)
    xc = xf - mu
    return xc * lax.rsqrt(jnp.mean(xc * xc, axis=-1, keepdims=True) + LN_EPS)


def _ln(xf, g, b):
    return _norm_rows(xf) * g + b


def _silu(x):
    return x * jax.nn.sigmoid(x)


def _softplus(x):
    return jnp.maximum(x, 0.0) + jnp.log1p(jnp.exp(-jnp.abs(x)))


def _scan_rows(a, tri):
    hi = a.astype(bf16)
    r1 = a - hi.astype(f32)
    mid = r1.astype(bf16)
    lo = (r1 - mid.astype(f32)).astype(bf16)
    cs = _dot(jnp.concatenate([hi, mid, lo], axis=0), tri)
    return cs[0:16] + cs[16:32] + cs[32:48]


def _dir_rows(dt, a_col, fwd):
    k = lax.broadcasted_iota(jnp.int32, (CHUNK, CHUNK), 0)
    j = lax.broadcasted_iota(jnp.int32, (CHUNK, CHUNK), 1)
    tri = jnp.where((k <= j) if fwd else (k >= j), 1.0, 0.0).astype(bf16)
    cum = _scan_rows(dt * a_col, tri)
    total = cum[:, CHUNK - 1:CHUNK] if fwd else cum[:, 0:1]
    return cum, jnp.exp(total - cum) * dt, jnp.exp(cum)


def _to_cols(rows):
    pad = jnp.zeros((LANES - 16 * len(rows), CHUNK), f32)
    return jnp.concatenate(list(rows) + [pad], axis=0).T


def _expand_heads(m):
    lane = lax.broadcasted_iota(jnp.int32, (CHUNK, LANES), 1)
    pieces = []
    for p in range(HEADS // 2):
        pieces.append(jnp.where(lane < HEAD_DIM, m[:, 2 * p:2 * p + 1], m[:, 2 * p + 1:2 * p + 2]))
    return jnp.concatenate(pieces, axis=1)


def _ssd_prep(dt, a_col, fwd):
    cum, wdt, ecum = _dir_rows(dt, a_col, fwd)
    return cum, _to_cols([cum, wdt, ecum])


def _ssd_chunk(prep, xs_bf, xs_f, bm, cm, dt, h_ref, fwd, want_y=True, fillers=()):
    cum, cols = prep
    fillers = list(fillers)
    ecum_x = _expand_heads(cols[:, 32:48])
    y = None
    if want_y:
        ii = lax.broadcasted_iota(jnp.int32, (CHUNK, CHUNK), 0)
        jj = lax.broadcasted_iota(jnp.int32, (CHUNK, CHUNK), 1)
        keep = (ii >= jj) if fwd else (ii <= jj)
        lane = lax.broadcasted_iota(jnp.int32, (CHUNK, LANES), 1)
        cb = [_dot_nt(cm[:, g * STATE:(g + 1) * STATE], bm[:, g * STATE:(g + 1) * STATE]) for g in range(GROUPS)]
        pieces = []
        for p in range(HEADS // 2):
            g = (2 * p) // (HEADS // GROUPS)
            ws = []
            for h in (2 * p, 2 * p + 1):
                seg = cols[:, h:h + 1] - cum[h:h + 1, :]
                ws.append(jnp.where(keep, cb[g] * jnp.exp(seg) * dt[h:h + 1, :], 0.0))
            lhs = jnp.concatenate(ws, axis=1).astype(bf16)
            x2 = xs_bf[:, p * LANES:(p + 1) * LANES]
            zero = jnp.zeros_like(x2)
            rhs = jnp.concatenate([jnp.where(lane < HEAD_DIM, x2, zero),
                                   jnp.where(lane >= HEAD_DIM, x2, zero)], axis=0)
            pieces.append(_dot(lhs, rhs))
            if fillers:
                fillers.pop(0)()
        y_diag = jnp.concatenate(pieces, axis=1)
        y_off = jnp.concatenate([_dot(cm[:, g * STATE:(g + 1) * STATE], h_ref[g].astype(bf16))
                                 for g in range(GROUPS)], axis=1)
        y = y_diag + y_off * ecum_x
    xw = (xs_f * _expand_heads(cols[:, 16:32])).astype(bf16)
    dec = ecum_x[CHUNK - 1:CHUNK, :] if fwd else ecum_x[0:1, :]
    for g in range(GROUPS):
        st = _dot_tn(bm[:, g * STATE:(g + 1) * STATE], xw[:, g * GROUP_W:(g + 1) * GROUP_W])
        h_ref[g] = dec[:, g * GROUP_W:(g + 1) * GROUP_W] * h_ref[g] + st
    for f in fillers:
        f()
    return y


def _conv_silu(ext_ref, convw_ref, convb_ref, t, c0, c1):
    n = t + 2 * HALO
    p = ext_ref[:, c0:c1]
    mid = CONV // 2
    acc = convb_ref[:, c0:c1] + p[HALO:HALO + t] * convw_ref[mid:mid + 1, c0:c1]
    for k in range(CONV):
        if k != mid:
            shifted = pltpu.roll(p, (mid - k) % n, axis=0)[HALO:HALO + t]
            acc = acc + shifted * convw_ref[k:k + 1, c0:c1]
    return _silu(acc)


def _mod_kernel(c_ref, w_ref, b_ref, o_ref):
    s = _silu(c_ref[...]).astype(bf16)
    o_ref[...] = _dot(s, w_ref[...].astype(bf16)) + b_ref[...]


def _mod_call(cc, w_ada, b_ada):
    nb = 1024
    return pl.pallas_call(
        _mod_kernel,
        grid=(D_MOD // nb,),
        in_specs=[pl.BlockSpec((16, D), lambda n: (0, 0)),
                  pl.BlockSpec((D, nb), lambda n: (0, n)),
                  pl.BlockSpec((1, nb), lambda n: (0, n))],
        out_specs=pl.BlockSpec((16, nb), lambda n: (0, n)),
        out_shape=jax.ShapeDtypeStruct((16, D_MOD), f32),
        compiler_params=pltpu.CompilerParams(dimension_semantics=("arbitrary",), vmem_limit_bytes=VMEM_LIMIT),
        name="adaln_mod",
    )(cc, w_ada, b_ada)


def _ctx_kernel(ctx_ref, mod_ref, ln0g_ref, ln0b_ref, wxbc_ref, wdtT_ref, convw_ref, convb_ref, dtb_ref,
                alog_ref, sf_ref, sb_ref, ext_ref, xs_ref, bc_ref):
    cl = ctx_ref.shape[0]
    sh1 = mod_ref[:, 0:D]
    sc1 = mod_ref[:, D:2 * D]
    xm = (_ln(ctx_ref[...], ln0g_ref[...], ln0b_ref[...]) * (1.0 + sc1) + sh1).astype(bf16)
    ext_ref[pl.ds(0, HALO), :] = jnp.zeros((HALO, XBC), f32)
    ext_ref[pl.ds(HALO + cl, HALO), :] = jnp.zeros((HALO, XBC), f32)
    ext_ref[pl.ds(HALO, cl), :] = _dot(xm, wxbc_ref[...])
    for c0 in range(0, D, NBLK):
        xs_ref[:, c0:c0 + NBLK] = _conv_silu(ext_ref, convw_ref, convb_ref, cl, c0, c0 + NBLK)
    bc_ref[...] = _conv_silu(ext_ref, convw_ref, convb_ref, cl, D, XBC).astype(bf16)
    dt = _softplus(_dot_nt(wdtT_ref[...], xm) + dtb_ref[...])
    a_col = -jnp.exp(alog_ref[...])
    sf_ref[...] = jnp.zeros(sf_ref.shape, f32)
    sb_ref[...] = jnp.zeros(sb_ref.shape, f32)
    nch = cl // CHUNK
    for fwd, h_ref, order in ((True, sf_ref, range(nch)), (False, sb_ref, range(nch - 1, -1, -1))):
        r0 = 0 if fwd else HEADS
        for c in order:
            rows = slice(c * CHUNK, (c + 1) * CHUNK)
            dtc = dt[r0:r0 + HEADS, rows]
            _ssd_chunk(_ssd_prep(dtc, a_col[r0:r0 + HEADS], fwd), None, xs_ref[rows, :],
                       bc_ref[rows, 0:GROUPS * STATE], None, dtc, h_ref, fwd, want_y=False)


def _ctx_call(ctx, mod3, ln0g, ln0b, w_xbc, w_dtT, convw, convb, dtb_col, alog_col):
    bsz, cl, _ = ctx.shape
    full = lambda shape: pl.BlockSpec(shape, lambda b: (0,) * len(shape))
    st_shape = jax.ShapeDtypeStruct((bsz, GROUPS, STATE, GROUP_W), f32)
    st_spec = pl.BlockSpec((None, GROUPS, STATE, GROUP_W), lambda b: (b, 0, 0, 0))
    return pl.pallas_call(
        _ctx_kernel,
        grid=(bsz,),
        in_specs=[pl.BlockSpec((None, cl, D), lambda b: (b, 0, 0)),
                  pl.BlockSpec((None, 1, 2 * D), lambda b: (0, 0, 0)),
                  full((1, D)), full((1, D)), full((D, XBC)), full((2 * HEADS, D)),
                  full((CONV, XBC)), full((1, XBC)), full((2 * HEADS, 1)), full((2 * HEADS, 1))],
        out_specs=[st_spec, st_spec],
        out_shape=[st_shape, st_shape],
        scratch_shapes=[pltpu.VMEM((cl + 2 * HALO, XBC), f32), pltpu.VMEM((cl, D), f32),
                        pltpu.VMEM((cl, 2 * GROUPS * STATE), bf16)],
        compiler_params=pltpu.CompilerParams(dimension_semantics=("arbitrary",), vmem_limit_bytes=VMEM_LIMIT),
        name="ctx_states",
    )(ctx, mod3, ln0g, ln0b, w_xbc, w_dtT, convw, convb, dtb_col, alog_col)


def _pass1_kernel(x_ref, xp_ref, xn_ref, mod_ref, ln0g_ref, ln0b_ref, wmain_ref, wdtT_ref, convw_ref, convb_ref,
                  dtb_ref, alog_ref, sb_ref, gmg_ref, gmb_ref, ws_ref, bsx_ref, bg_ref,
                  x0a_ref, sz_ref, xs_ref, bc_ref, yb_ref, ygm_ref, gate_ref, dt_ref,
                  ext_ref, hb_ref, xsf_ref, ug_ref, vg_ref, vn_ref):
    t = x_ref.shape[0]
    nch = t // CHUNK
    step = pl.program_id(1)
    nt = pl.num_programs(1)
    tile = nt - 1 - step

    @pl.when(step == 0)
    def _():
        hb_ref[...] = sb_ref[...]

    g0 = ln0g_ref[...]
    b0 = ln0b_ref[...]
    sc1p = 1.0 + mod_ref[:, D:2 * D]
    gm = g0 * sc1p
    bm = b0 * sc1p + mod_ref[:, 0:D]

    xn = _norm_rows(x_ref[...])
    x0a_ref[...] = xn * (ALPHA * g0) + ALPHA * b0
    xm_f = xn * gm + bm
    xm = xm_f.astype(bf16)
    ext = jnp.concatenate([_norm_rows(xp_ref[...]) * gm + bm, xm_f, _norm_rows(xn_ref[...]) * gm + bm],
                          axis=0).astype(bf16)
    xbc_ext = _dot(ext, wmain_ref[:, OFF_XBC:OFF_U])
    row = lax.broadcasted_iota(jnp.int32, (t + 2 * HALO, 1), 0)
    valid = jnp.logical_and(jnp.logical_or(row >= HALO, tile > 0), jnp.logical_or(row < HALO + t, tile < nt - 1))
    ext_ref[...] = jnp.where(valid, xbc_ext, 0.0)

    dt = _softplus(_dot_nt(wdtT_ref[...], xm) + dtb_ref[...])
    dt_ref[...] = dt
    a_col = -jnp.exp(alog_ref[...])

    def proj_task(off, c0, dst_ref, fn):
        def run():
            cols = slice(c0, c0 + NBLK)
            dst_ref[:, cols] = fn(_dot(xm, wmain_ref[:, off + c0:off + c0 + NBLK]), cols).astype(dst_ref.dtype)
        return run

    def conv_task(c0):
        def run():
            blk = _conv_silu(ext_ref, convw_ref, convb_ref, t, c0, c0 + LANES)
            if c0 < D:
                xsf_ref[:, c0:c0 + LANES] = blk
                xs_ref[:, c0:c0 + LANES] = blk.astype(bf16)
            else:
                bc_ref[:, c0 - D:c0 - D + LANES] = blk.astype(bf16)
        return run

    blocks = range(0, D, NBLK)
    proj_tasks = ([proj_task(OFF_Z, c0, sz_ref, lambda r, cols: _silu(r)) for c0 in blocks]
                  + [proj_task(OFF_U, c0, ug_ref, lambda r, cols: jax.nn.gelu(r)) for c0 in blocks]
                  + [proj_task(OFF_V, c0, vg_ref, lambda r, cols: jax.nn.gelu(r)) for c0 in blocks])
    gate_tasks = [proj_task(OFF_G, c0, gate_ref, lambda r, cols: jax.nn.sigmoid(r + bg_ref[:, cols]))
                  for c0 in range(0, 2 * D, NBLK)]
    conv_tasks = [conv_task(c0) for c0 in range(0, XBC, LANES)]

    per = len(conv_tasks) // len(proj_tasks)
    for i, task in enumerate(proj_tasks):
        task()
        for conv in conv_tasks[i * per:(i + 1) * per]:
            conv()

    preps = {c: _ssd_prep(dt[HEADS:, c * CHUNK:(c + 1) * CHUNK], a_col[HEADS:], False) for c in range(nch)}
    per = len(gate_tasks) // nch
    for i, c in enumerate(range(nch - 1, -1, -1)):
        rows = slice(c * CHUNK, (c + 1) * CHUNK)
        vn_ref[rows, :] = _ln(vg_ref[rows, :], gmg_ref[...], gmb_ref[...]).astype(bf16)
        y = _ssd_chunk(preps[c], xs_ref[rows, :], xsf_ref[rows, :], bc_ref[rows, 0:GROUPS * STATE],
                       bc_ref[rows, GROUPS * STATE:], dt[HEADS:, rows], hb_ref, fwd=False,
                       fillers=gate_tasks[i * per:(i + 1) * per])
        yb_ref[rows, :] = y.astype(bf16)

    for g in range(GM_GROUPS):
        cols = slice(g * GM_GROUP_DIM, (g + 1) * GM_GROUP_DIM)
        rhs = jnp.concatenate([vn_ref[c * CHUNK:(c + 1) * CHUNK, cols] for c in range(nch)], axis=1)
        mixed = _dot(ws_ref[g], rhs)
        for c in range(nch):
            rows = slice(c * CHUNK, (c + 1) * CHUNK)
            ygm_ref[rows, cols] = (ug_ref[rows, cols].astype(f32)
                                   * (mixed[:, c * GM_GROUP_DIM:(c + 1) * GM_GROUP_DIM] + bsx_ref[:, cols])).astype(bf16)


def _pass1_call(x, mod3, ln0g, ln0b, w_main, w_dtT, convw, convb, dtb_col, alog_col, s_b, gmg, gmb, ws, bsx, bg, t):
    bsz, seq, _ = x.shape
    nt = seq // t
    hb = t // HALO
    nhb = seq // HALO
    const = lambda shape: pl.BlockSpec(shape, lambda b, s: (0,) * len(shape), pipeline_mode=pl.Buffered(1))
    tok = lambda w: pl.BlockSpec((None, t, w), lambda b, s: (b, nt - 1 - s, 0))
    act = lambda w, dt_: jax.ShapeDtypeStruct((bsz, seq, w), dt_)
    return pl.pallas_call(
        _pass1_kernel,
        grid=(bsz, nt),
        in_specs=[tok(D),
                  pl.BlockSpec((None, HALO, D), lambda b, s: (b, jnp.maximum((nt - 1 - s) * hb - 1, 0), 0)),
                  pl.BlockSpec((None, HALO, D), lambda b, s: (b, jnp.minimum((nt - s) * hb, nhb - 1), 0)),
                  pl.BlockSpec((None, 1, 2 * D), lambda b, s: (b, 0, 0)),
                  const((1, D)), const((1, D)), const((D, W_MAIN)), const((2 * HEADS, D)),
                  const((CONV, XBC)), const((1, XBC)), const((2 * HEADS, 1)), const((2 * HEADS, 1)),
                  pl.BlockSpec((None, GROUPS, STATE, GROUP_W), lambda b, s: (b, 0, 0, 0)),
                  const((1, D)), const((1, D)), const((GM_GROUPS, CHUNK, CHUNK)), const((CHUNK, D)),
                  const((1, 2 * D))],
        out_specs=[tok(D), tok(D), tok(D), tok(2 * GROUPS * STATE), tok(D), tok(D), tok(2 * D),
                   pl.BlockSpec((None, 2 * HEADS, t), lambda b, s: (b, 0, nt - 1 - s))],
        out_shape=[act(D, f32), act(D, bf16), act(D, bf16), act(2 * GROUPS * STATE, bf16), act(D, bf16),
                   act(D, bf16), act(2 * D, bf16), jax.ShapeDtypeStruct((bsz, 2 * HEADS, seq), f32)],
        scratch_shapes=[pltpu.VMEM((t + 2 * HALO, XBC), f32), pltpu.VMEM((GROUPS, STATE, GROUP_W), f32),
                        pltpu.VMEM((t, D), f32), pltpu.VMEM((t, D), bf16), pltpu.VMEM((t, D), f32),
                        pltpu.VMEM((t, D), bf16)],
        compiler_params=pltpu.CompilerParams(dimension_semantics=("arbitrary", "arbitrary"),
                                             vmem_limit_bytes=VMEM_LIMIT),
        name="pass1_bwd",
    )(x, x, x, mod3, ln0g, ln0b, w_main, w_dtT, convw, convb, dtb_col, alog_col, s_b, gmg, gmb, ws, bsx, bg)


def _pass2_kernel(x0a_ref, mod_ref, sz_ref, xs_ref, bc_ref, yb_ref, ygm_ref, gate_ref, dt_ref,
                  alog_ref, dsk_ref, ng_ref, wsp_ref, wgp_ref, wo_ref, ln1g_ref, ln1b_ref, sf_ref,
                  o_ref, hf_ref, ys_ref, mg_ref):
    t = x0a_ref.shape[0]
    nch = t // CHUNK
    step = pl.program_id(1)

    @pl.when(step == 0)
    def _():
        hf_ref[...] = sf_ref[...]

    a_col = -jnp.exp(alog_ref[...])
    dt = dt_ref[...]
    preps = [_ssd_prep(dt[0:HEADS, c * CHUNK:(c + 1) * CHUNK], a_col[0:HEADS], True) for c in range(nch)]
    dsk = dsk_ref[0:1, :] + dsk_ref[1:2, :]

    def gm_task(c0):
        def run():
            cols = slice(c0, c0 + NBLK)
            mg_ref[:, cols] = (gate_ref[:, D + c0:D + c0 + NBLK].astype(f32)
                               * _dot(ygm_ref[...], wgp_ref[:, cols]))
        return run

    gm_tasks = [gm_task(c0) for c0 in range(0, D, NBLK)]
    per = len(gm_tasks) // nch
    for c in range(nch):
        rows = slice(c * CHUNK, (c + 1) * CHUNK)
        xs_bf = xs_ref[rows, :]
        xs_f = xs_bf.astype(f32)
        y = _ssd_chunk(preps[c], xs_bf, xs_f, bc_ref[rows, 0:GROUPS * STATE], bc_ref[rows, GROUPS * STATE:],
                       dt[0:HEADS, rows], hf_ref, fwd=True, fillers=gm_tasks[c * per:(c + 1) * per])
        y = y + yb_ref[rows, :].astype(f32) + xs_f * dsk
        hh = y * sz_ref[rows, :].astype(f32)
        hh = hh * lax.rsqrt(jnp.mean(hh * hh, axis=-1, keepdims=True) + LN_EPS) * ng_ref[...]
        ys_ref[rows, :] = hh.astype(bf16)

    merged = gate_ref[:, 0:D].astype(f32) * _dot(ys_ref[...], wsp_ref[...]) + mg_ref[...]
    out_x = _dot(merged.astype(bf16), wo_ref[...])
    g1 = mod_ref[:, 2 * D:3 * D]
    o_ref[...] = _ln(x0a_ref[...] + g1 * out_x, ln1g_ref[...], ln1b_ref[...])


def _pass2_call(x0a, mod3, sz, xs, bc, yb, ygm, gate, dt, alog_col, dsk, ng, wsp, wgp, wo, ln1g, ln1b, s_f, t):
    bsz, seq, _ = x0a.shape
    nt = seq // t
    const = lambda shape: pl.BlockSpec(shape, lambda b, s: (0,) * len(shape), pipeline_mode=pl.Buffered(1))
    tok = lambda w: pl.BlockSpec((None, t, w), lambda b, s: (b, s, 0))
    return pl.pallas_call(
        _pass2_kernel,
        grid=(bsz, nt),
        in_specs=[tok(D), pl.BlockSpec((None, 1, D_MOD), lambda b, s: (b, 0, 0)),
                  tok(D), tok(D), tok(2 * GROUPS * STATE), tok(D), tok(D), tok(2 * D),
                  pl.BlockSpec((None, 2 * HEADS, t), lambda b, s: (b, 0, s)),
                  const((2 * HEADS, 1)), const((2, D)), const((1, D)),
                  const((D, D)), const((D, D)), const((D, D)), const((1, D)), const((1, D)),
                  pl.BlockSpec((None, GROUPS, STATE, GROUP_W), lambda b, s: (b, 0, 0, 0))],
        out_specs=tok(D),
        out_shape=jax.ShapeDtypeStruct((bsz, seq, D), f32),
        scratch_shapes=[pltpu.VMEM((GROUPS, STATE, GROUP_W), f32), pltpu.VMEM((t, D), bf16),
                        pltpu.VMEM((t, D), f32)],
        compiler_params=pltpu.CompilerParams(dimension_semantics=("arbitrary", "arbitrary"),
                                             vmem_limit_bytes=VMEM_LIMIT),
        name="pass2_fwd",
    )(x0a, mod3, sz, xs, bc, yb, ygm, gate, dt, alog_col, dsk, ng, wsp, wgp, wo, ln1g, ln1b, s_f)


def _ffn_kernel(x_ref, mod_ref, w1_ref, w3_ref, w2_ref, ln2g_ref, ln2b_ref, o_ref, a_ref):
    x1 = x_ref[...]
    sh2 = mod_ref[:, 3 * D:4 * D]
    sc2 = mod_ref[:, 4 * D:5 * D]
    g2 = mod_ref[:, 5 * D:6 * D]
    hm = (x1 * (1.0 + sc2) + sh2).astype(bf16)
    nb = 256
    for n0 in range(0, D_FF, nb):
        h1 = _dot(hm, w1_ref[:, n0:n0 + nb])
        h3 = _dot(hm, w3_ref[:, n0:n0 + nb])
        a_ref[:, n0:n0 + nb] = (_silu(h1) * h3).astype(bf16)
    ff = _dot(a_ref[...], w2_ref[...])
    o_ref[...] = _ln(ALPHA * x1 + g2 * ff, ln2g_ref[...], ln2b_ref[...])


def _ffn_call(x1, mod3, w1, w3, w2, ln2g, ln2b, t):
    bsz, seq, _ = x1.shape
    const = lambda shape: pl.BlockSpec(shape, lambda b, s: (0,) * len(shape), pipeline_mode=pl.Buffered(1))
    tok = pl.BlockSpec((None, t, D), lambda b, s: (b, s, 0))
    return pl.pallas_call(
        _ffn_kernel,
        grid=(bsz, seq // t),
        in_specs=[tok, pl.BlockSpec((None, 1, D_MOD), lambda b, s: (b, 0, 0)),
                  const((D, D_FF)), const((D, D_FF)), const((D_FF, D)), const((1, D)), const((1, D))],
        out_specs=tok,
        out_shape=jax.ShapeDtypeStruct((bsz, seq, D), f32),
        scratch_shapes=[pltpu.VMEM((t, D_FF), bf16)],
        compiler_params=pltpu.CompilerParams(dimension_semantics=("arbitrary", "arbitrary"),
                                             vmem_limit_bytes=VMEM_LIMIT),
        name="ffn",
    )(x1, mod3, w1, w3, w2, ln2g, ln2b)


def kernel(x, c, ctx, c_ctx, ln0_g, ln0_b, w_ada, b_ada, w_in, conv_w, conv_b, dt_bias, a_log, d_skip, ssd_norm_g, gm_norm_g, gm_norm_b, w_spatial, b_spatial, b_gate, w_ssd_proj, w_gm_proj, w_out, ln1_g, ln1_b, w_ff1, w_ff3, w_ff2, ln2_g, ln2_b):
    bsz, seq, _ = x.shape
    assert x.shape[2] == D and w_in.shape == (DEPTH, D, W_MAIN + 2 * HEADS) and bsz < 16
    assert seq % 512 == 0 and ctx.shape[1] % CHUNK == 0
    row = lambda a: a.reshape(1, -1)

    w_in0 = w_in[0]
    o_dt = D + XBC
    w_main = jnp.concatenate([w_in0[:, :o_dt], w_in0[:, o_dt + 2 * HEADS:]], axis=1).astype(bf16)
    w_xbc = w_in0[:, D:o_dt].astype(bf16)
    w_dtT = w_in0[:, o_dt:o_dt + 2 * HEADS].T.astype(bf16)
    dtb_col = dt_bias[0].reshape(2 * HEADS, 1)
    alog_col = a_log[0].reshape(2 * HEADS, 1)
    dsk = jnp.repeat(d_skip[0], HEAD_DIM, axis=1)
    bsx = jnp.repeat(b_spatial[0].T, GM_GROUP_DIM, axis=1)
    ln0g, ln0b = row(ln0_g), row(ln0_b)

    cc = jnp.concatenate([c, c_ctx[None, :], jnp.zeros((15 - bsz, D), f32)], axis=0)
    mod = _mod_call(cc, w_ada[0], row(b_ada[0]))
    mod3 = mod.reshape(16, 1, D_MOD)
    s_f, s_b = _ctx_call(ctx, mod3[bsz:bsz + 1], ln0g, ln0b, w_xbc, w_dtT,
                         conv_w[0], row(conv_b[0]), dtb_col, alog_col)

    t = 256
    x0a, sz, xs, bc, yb, ygm, gate, dt = _pass1_call(
        x, mod3, ln0g, ln0b, w_main, w_dtT, conv_w[0], row(conv_b[0]), dtb_col, alog_col, s_b,
        row(gm_norm_g[0]), row(gm_norm_b[0]), w_spatial[0].astype(bf16), bsx, row(b_gate[0]), t)
    x1 = _pass2_call(x0a, mod3, sz, xs, bc, yb, ygm, gate, dt, alog_col, dsk, row(ssd_norm_g[0]),
                     w_ssd_proj[0].astype(bf16), w_gm_proj[0].astype(bf16), w_out[0].astype(bf16),
                     row(ln1_g[0]), row(ln1_b[0]), s_f, t)
    return _ffn_call(x1, mod3, w_ff1[0].astype(bf16), w_ff3[0].astype(bf16), w_ff2[0].astype(bf16),
                     row(ln2_g[0]), row(ln2_b[0]), 512)
```

```python
import functools

import jax
import jax.numpy as jnp
from jax import lax
from jax.experimental import pallas as pl
from jax.experimental.pallas import tpu as pltpu

f32 = jnp.float32
bf16 = jnp.bfloat16

D = 1024
HEADS = 16
HEAD_DIM = 64
GROUPS = 2
GROUP_W = D // GROUPS
STATE = 128
CONV = 5
CHUNK = 128
XBC = D + 2 * GROUPS * STATE
GM_GROUPS = 8
GM_GROUP_DIM = D // GM_GROUPS
D_FF = 2816
D_MOD = 6 * D
DEPTH = 1
ALPHA = (2 * DEPTH) ** 0.25
LN_EPS = 1e-5
HALO = 8
LANES = 128
NBLK = 256

OFF_Z, OFF_XBC, OFF_U, OFF_V, OFF_G, W_MAIN = 0, 1024, 2560, 3584, 4608, 6656

VMEM_LIMIT = 56 * 1024 * 1024


def _dot(a, b):
    return jnp.dot(a, b, preferred_element_type=f32)


def _dot_nt(a, b):
    return lax.dot_general(a, b, (((1,), (1,)), ((), ())), preferred_element_type=f32)


def _dot_tn(a, b):
    return lax.dot_general(a, b, (((0,), (0,)), ((), ())), preferred_element_type=f32)


def _norm_rows(xf):
    mu = jnp.mean(xf, axis=-1, keepdims=True)
    xc = xf - mu
    return xc * lax.rsqrt(jnp.mean(xc * xc, axis=-1, keepdims=True) + LN_EPS)


def _ln(xf, g, b):
    return _norm_rows(xf) * g + b


def _sigmoid(x):
    return 0.5 * jnp.tanh(0.5 * x) + 0.5


def _silu(x):
    h = 0.5 * x
    return h * jnp.tanh(h) + h


def _softplus(x):
    return jnp.maximum(x, 0.0) + jnp.log1p(jnp.exp(-jnp.abs(x)))


def _scan_rows(a, tri):
    hi = a.astype(bf16)
    r1 = a - hi.astype(f32)
    mid = r1.astype(bf16)
    lo = (r1 - mid.astype(f32)).astype(bf16)
    cs = _dot(jnp.concatenate([hi, mid, lo], axis=0), tri)
    return cs[0:16] + cs[16:32] + cs[32:48]


def _dir_rows(dt, a_col, fwd):
    k = lax.broadcasted_iota(jnp.int32, (CHUNK, CHUNK), 0)
    j = lax.broadcasted_iota(jnp.int32, (CHUNK, CHUNK), 1)
    tri = jnp.where((k <= j) if fwd else (k >= j), 1.0, 0.0).astype(bf16)
    cum = _scan_rows(dt * a_col, tri)
    total = cum[:, CHUNK - 1:CHUNK] if fwd else cum[:, 0:1]
    return cum, jnp.exp(total - cum) * dt, jnp.exp(cum)


def _to_cols(rows):
    pad = jnp.zeros((LANES - 16 * len(rows), CHUNK), f32)
    return jnp.concatenate(list(rows) + [pad], axis=0).T


def _expand_heads(m):
    lane = lax.broadcasted_iota(jnp.int32, (CHUNK, LANES), 1)
    pieces = []
    for p in range(HEADS // 2):
        pieces.append(jnp.where(lane < HEAD_DIM, m[:, 2 * p:2 * p + 1], m[:, 2 * p + 1:2 * p + 2]))
    return jnp.concatenate(pieces, axis=1)


def _ssd_prep(dt, a_col, fwd):
    cum, wdt, ecum = _dir_rows(dt, a_col, fwd)
    return cum - jnp.log(dt), _to_cols([cum, wdt, ecum])


def _ssd_chunk(prep, xs_bf, xs_f, bm, cm, h_ref, fwd, want_y=True, fillers=()):
    cumj, cols = prep
    fillers = list(fillers)
    ecum_x = _expand_heads(cols[:, 32:48])
    y = None
    if want_y:
        ii = lax.broadcasted_iota(jnp.int32, (CHUNK, CHUNK), 0)
        jj = lax.broadcasted_iota(jnp.int32, (CHUNK, CHUNK), 1)
        keep = (ii >= jj) if fwd else (ii <= jj)
        lane = lax.broadcasted_iota(jnp.int32, (CHUNK, LANES), 1)
        cb = [_dot_nt(cm[:, g * STATE:(g + 1) * STATE], bm[:, g * STATE:(g + 1) * STATE]) for g in range(GROUPS)]
        pieces = []
        for p in range(HEADS // 2):
            g = (2 * p) // (HEADS // GROUPS)
            ws = []
            for h in (2 * p, 2 * p + 1):
                seg = cols[:, h:h + 1] - cumj[h:h + 1, :]
                ws.append(jnp.where(keep, cb[g] * jnp.exp(seg), 0.0))
            lhs = jnp.concatenate(ws, axis=1).astype(bf16)
            x2 = xs_bf[:, p * LANES:(p + 1) * LANES]
            zero = jnp.zeros_like(x2)
            rhs = jnp.concatenate([jnp.where(lane < HEAD_DIM, x2, zero),
                                   jnp.where(lane >= HEAD_DIM, x2, zero)], axis=0)
            pieces.append(_dot(lhs, rhs))
            if fillers:
                fillers.pop(0)()
        y_diag = jnp.concatenate(pieces, axis=1)
        y_off = jnp.concatenate([_dot(cm[:, g * STATE:(g + 1) * STATE], h_ref[g].astype(bf16))
                                 for g in range(GROUPS)], axis=1)
        y = y_diag + y_off * ecum_x
    xw = (xs_f * _expand_heads(cols[:, 16:32])).astype(bf16)
    dec = ecum_x[CHUNK - 1:CHUNK, :] if fwd else ecum_x[0:1, :]
    for g in range(GROUPS):
        st = _dot_tn(bm[:, g * STATE:(g + 1) * STATE], xw[:, g * GROUP_W:(g + 1) * GROUP_W])
        h_ref[g] = dec[:, g * GROUP_W:(g + 1) * GROUP_W] * h_ref[g] + st
    for f in fillers:
        f()
    return y


def _conv_silu(ext_ref, convw_ref, convb_ref, t, c0, c1):
    n = (t + 2 * HALO) // 8
    w = c1 - c0
    p = ext_ref[:, c0:c1].reshape(n, 8, w)
    sub = lax.broadcasted_iota(jnp.int32, (1, 8, w), 1)
    mid = CONV // 2
    acc = convb_ref[:, c0:c1].reshape(1, 1, w) + p[1:n - 1] * convw_ref[mid:mid + 1, c0:c1].reshape(1, 1, w)
    for k in range(CONV):
        s = k - mid
        if s > 0:
            mixed = jnp.where(sub >= s, p[1:n - 1], p[2:n])
            shifted = pltpu.roll(mixed, 8 - s, axis=1)
        elif s < 0:
            mixed = jnp.where(sub < 8 + s, p[1:n - 1], p[0:n - 2])
            shifted = pltpu.roll(mixed, -s, axis=1)
        else:
            continue
        acc = acc + shifted * convw_ref[k:k + 1, c0:c1].reshape(1, 1, w)
    return _silu(acc).reshape(t, w)


def _mod_kernel(c_ref, w_ref, b_ref, o_ref):
    s = _silu(c_ref[...]).astype(bf16)
    o_ref[...] = _dot(s, w_ref[...].astype(bf16)) + b_ref[...]


def _mod_call(cc, w_ada, b_ada):
    nb = 1024
    return pl.pallas_call(
        _mod_kernel,
        grid=(D_MOD // nb,),
        in_specs=[pl.BlockSpec((16, D), lambda n: (0, 0)),
                  pl.BlockSpec((D, nb), lambda n: (0, n)),
                  pl.BlockSpec((1, nb), lambda n: (0, n))],
        out_specs=pl.BlockSpec((16, nb), lambda n: (0, n)),
        out_shape=jax.ShapeDtypeStruct((16, D_MOD), f32),
        compiler_params=pltpu.CompilerParams(dimension_semantics=("arbitrary",), vmem_limit_bytes=VMEM_LIMIT),
        name="adaln_mod",
    )(cc, w_ada, b_ada)


def _ctx_kernel(ctx_ref, mod_ref, ln0g_ref, ln0b_ref, wxbc_ref, wdtT_ref, convw_ref, convb_ref, dtb_ref,
                alog_ref, sf_ref, sb_ref, ext_ref, xs_ref, bc_ref):
    cl = ctx_ref.shape[0]
    sh1 = mod_ref[:, 0:D]
    sc1 = mod_ref[:, D:2 * D]
    xm = (_ln(ctx_ref[...], ln0g_ref[...], ln0b_ref[...]) * (1.0 + sc1) + sh1).astype(bf16)
    ext_ref[pl.ds(0, HALO), :] = jnp.zeros((HALO, XBC), f32)
    ext_ref[pl.ds(HALO + cl, HALO), :] = jnp.zeros((HALO, XBC), f32)
    ext_ref[pl.ds(HALO, cl), :] = _dot(xm, wxbc_ref[...])
    for c0 in range(0, D, NBLK):
        xs_ref[:, c0:c0 + NBLK] = _conv_silu(ext_ref, convw_ref, convb_ref, cl, c0, c0 + NBLK)
    bc_ref[...] = _conv_silu(ext_ref, convw_ref, convb_ref, cl, D, XBC).astype(bf16)
    dt = _softplus(_dot_nt(wdtT_ref[...], xm) + dtb_ref[...])
    a_col = -jnp.exp(alog_ref[...])
    sf_ref[...] = jnp.zeros(sf_ref.shape, f32)
    sb_ref[...] = jnp.zeros(sb_ref.shape, f32)
    nch = cl // CHUNK
    for fwd, h_ref, order in ((True, sf_ref, range(nch)), (False, sb_ref, range(nch - 1, -1, -1))):
        r0 = 0 if fwd else HEADS
        for c in order:
            rows = slice(c * CHUNK, (c + 1) * CHUNK)
            dtc = dt[r0:r0 + HEADS, rows]
            _ssd_chunk(_ssd_prep(dtc, a_col[r0:r0 + HEADS], fwd), None, xs_ref[rows, :],
                       bc_ref[rows, 0:GROUPS * STATE], None, h_ref, fwd, want_y=False)


def _ctx_call(ctx, mod3, ln0g, ln0b, w_xbc, w_dtT, convw, convb, dtb_col, alog_col):
    bsz, cl, _ = ctx.shape
    full = lambda shape: pl.BlockSpec(shape, lambda b: (0,) * len(shape))
    st_shape = jax.ShapeDtypeStruct((bsz, GROUPS, STATE, GROUP_W), f32)
    st_spec = pl.BlockSpec((None, GROUPS, STATE, GROUP_W), lambda b: (b, 0, 0, 0))
    return pl.pallas_call(
        _ctx_kernel,
        grid=(bsz,),
        in_specs=[pl.BlockSpec((None, cl, D), lambda b: (b, 0, 0)),
                  pl.BlockSpec((None, 1, 2 * D), lambda b: (0, 0, 0)),
                  full((1, D)), full((1, D)), full((D, XBC)), full((2 * HEADS, D)),
                  full((CONV, XBC)), full((1, XBC)), full((2 * HEADS, 1)), full((2 * HEADS, 1))],
        out_specs=[st_spec, st_spec],
        out_shape=[st_shape, st_shape],
        scratch_shapes=[pltpu.VMEM((cl + 2 * HALO, XBC), f32), pltpu.VMEM((cl, D), f32),
                        pltpu.VMEM((cl, 2 * GROUPS * STATE), bf16)],
        compiler_params=pltpu.CompilerParams(dimension_semantics=("arbitrary",), vmem_limit_bytes=VMEM_LIMIT),
        name="ctx_states",
    )(ctx, mod3, ln0g, ln0b, w_xbc, w_dtT, convw, convb, dtb_col, alog_col)


def _pass1_kernel(x_ref, xp_ref, xn_ref, mod_ref, ln0g_ref, ln0b_ref, wmain_ref, wdtT_ref, convw_ref, convb_ref,
                  dtb_ref, alog_ref, sb_ref, gmg_ref, gmb_ref, ws_ref, bsx_ref, bg_ref,
                  x0a_ref, sz_ref, xs_ref, bc_ref, yb_ref, ygm_ref, gate_ref, dt_ref,
                  ext_ref, hb_ref, xsf_ref, ug_ref, vg_ref, vn_ref):
    t = x_ref.shape[0]
    nch = t // CHUNK
    step = pl.program_id(1)
    nt = pl.num_programs(1)
    tile = nt - 1 - step

    @pl.when(step == 0)
    def _():
        hb_ref[...] = sb_ref[...]

    g0 = ln0g_ref[...]
    b0 = ln0b_ref[...]
    sc1p = 1.0 + mod_ref[:, D:2 * D]
    gm = g0 * sc1p
    bm = b0 * sc1p + mod_ref[:, 0:D]

    xn = _norm_rows(x_ref[...])
    x0a_ref[...] = xn * (ALPHA * g0) + ALPHA * b0
    xm_f = xn * gm + bm
    xm = xm_f.astype(bf16)
    ext = jnp.concatenate([_norm_rows(xp_ref[...]) * gm + bm, xm_f, _norm_rows(xn_ref[...]) * gm + bm],
                          axis=0).astype(bf16)
    row = lax.broadcasted_iota(jnp.int32, (t + 2 * HALO, 1), 0)
    valid = jnp.logical_and(jnp.logical_or(row >= HALO, tile > 0), jnp.logical_or(row < HALO + t, tile < nt - 1))

    def ext_task(c0):
        def run():
            w = XBC // 3
            ext_ref[:, c0:c0 + w] = jnp.where(valid, _dot(ext, wmain_ref[:, OFF_XBC + c0:OFF_XBC + c0 + w]), 0.0)
        return run

    def proj_task(off, c0, dst_ref, fn):
        def run():
            cols = slice(c0, c0 + NBLK)
            dst_ref[:, cols] = fn(_dot(xm, wmain_ref[:, off + c0:off + c0 + NBLK]), cols).astype(dst_ref.dtype)
        return run

    def conv_task(c0):
        def run():
            blk = _conv_silu(ext_ref, convw_ref, convb_ref, t, c0, c0 + LANES)
            if c0 < D:
                xsf_ref[:, c0:c0 + LANES] = blk
                xs_ref[:, c0:c0 + LANES] = blk.astype(bf16)
            else:
                bc_ref[:, c0 - D:c0 - D + LANES] = blk.astype(bf16)
        return run

    def vnorm_task(c):
        def run():
            rows = slice(c * CHUNK, (c + 1) * CHUNK)
            vn_ref[rows, :] = _ln(vg_ref[rows, :], gmg_ref[...], gmb_ref[...]).astype(bf16)
        return run

    def gating_task(g):
        def run():
            cols = slice(g * GM_GROUP_DIM, (g + 1) * GM_GROUP_DIM)
            rhs = jnp.concatenate([vn_ref[c * CHUNK:(c + 1) * CHUNK, cols] for c in range(nch)], axis=1)
            mixed = _dot(ws_ref[g], rhs)
            for c in range(nch):
                rows = slice(c * CHUNK, (c + 1) * CHUNK)
                ygm_ref[rows, cols] = (ug_ref[rows, cols].astype(f32)
                                       * (mixed[:, c * GM_GROUP_DIM:(c + 1) * GM_GROUP_DIM]
                                          + bsx_ref[:, cols])).astype(bf16)
        return run

    blocks = range(0, D, NBLK)
    v_t = [proj_task(OFF_V, c0, vg_ref, lambda r, cols: jax.nn.gelu(r)) for c0 in blocks]
    u_t = [proj_task(OFF_U, c0, ug_ref, lambda r, cols: jax.nn.gelu(r)) for c0 in blocks]
    z_t = [proj_task(OFF_Z, c0, sz_ref, lambda r, cols: _silu(r)) for c0 in blocks]
    g_t = [proj_task(OFF_G, c0, gate_ref, lambda r, cols: _sigmoid(r + bg_ref[:, cols]))
           for c0 in range(0, 2 * D, NBLK)]
    e_t = [ext_task(c0) for c0 in range(0, XBC, XBC // 3)]
    c_t = [conv_task(c0) for c0 in range(0, XBC, LANES)]
    n_t = [vnorm_task(c) for c in range(nch)]
    s_t = [gating_task(g) for g in range(GM_GROUPS)]
    assert (len(v_t), len(c_t), len(g_t), nch) == (4, 12, 8, 2)

    for task in (v_t[0], e_t[0], v_t[1], e_t[1], v_t[2], e_t[2], v_t[3]):
        task()
    dt = _softplus(_dot_nt(wdtT_ref[...], xm) + dtb_ref[...])
    dt_ref[...] = dt
    a_col = -jnp.exp(alog_ref[...])
    preps = [_ssd_prep(dt[HEADS:, c * CHUNK:(c + 1) * CHUNK], a_col[HEADS:], False) for c in range(nch)]
    for task in (u_t[0], c_t[0], u_t[1], c_t[1], n_t[0], u_t[2], c_t[2], n_t[1], u_t[3], c_t[3]):
        task()
    for i in range(4):
        for task in (z_t[i], c_t[4 + 2 * i], g_t[i], c_t[5 + 2 * i], s_t[2 * i], s_t[2 * i + 1]):
            task()
    for i, c in enumerate(range(nch - 1, -1, -1)):
        rows = slice(c * CHUNK, (c + 1) * CHUNK)
        y = _ssd_chunk(preps[c], xs_ref[rows, :], xsf_ref[rows, :], bc_ref[rows, 0:GROUPS * STATE],
                       bc_ref[rows, GROUPS * STATE:], hb_ref, fwd=False, fillers=g_t[4 + 2 * i:6 + 2 * i])
        yb_ref[rows, :] = y.astype(bf16)


def _pass1_call(x, mod3, ln0g, ln0b, w_main, w_dtT, convw, convb, dtb_col, alog_col, s_b, gmg, gmb, ws, bsx, bg, t):
    bsz, seq, _ = x.shape
    nt = seq // t
    hb = t // HALO
    nhb = seq // HALO
    const = lambda shape: pl.BlockSpec(shape, lambda b, s: (0,) * len(shape), pipeline_mode=pl.Buffered(1))
    tok = lambda w: pl.BlockSpec((None, t, w), lambda b, s: (b, nt - 1 - s, 0))
    act = lambda w, dt_: jax.ShapeDtypeStruct((bsz, seq, w), dt_)
    return pl.pallas_call(
        _pass1_kernel,
        grid=(bsz, nt),
        in_specs=[tok(D),
                  pl.BlockSpec((None, HALO, D), lambda b, s: (b, jnp.maximum((nt - 1 - s) * hb - 1, 0), 0)),
                  pl.BlockSpec((None, HALO, D), lambda b, s: (b, jnp.minimum((nt - s) * hb, nhb - 1), 0)),
                  pl.BlockSpec((None, 1, 2 * D), lambda b, s: (b, 0, 0)),
                  const((1, D)), const((1, D)), const((D, W_MAIN)), const((2 * HEADS, D)),
                  const((CONV, XBC)), const((1, XBC)), const((2 * HEADS, 1)), const((2 * HEADS, 1)),
                  pl.BlockSpec((None, GROUPS, STATE, GROUP_W), lambda b, s: (b, 0, 0, 0)),
                  const((1, D)), const((1, D)), const((GM_GROUPS, CHUNK, CHUNK)), const((CHUNK, D)),
                  const((1, 2 * D))],
        out_specs=[tok(D), tok(D), tok(D), tok(2 * GROUPS * STATE), tok(D), tok(D), tok(2 * D),
                   pl.BlockSpec((None, 2 * HEADS, t), lambda b, s: (b, 0, nt - 1 - s))],
        out_shape=[act(D, f32), act(D, bf16), act(D, bf16), act(2 * GROUPS * STATE, bf16), act(D, bf16),
                   act(D, bf16), act(2 * D, bf16), jax.ShapeDtypeStruct((bsz, 2 * HEADS, seq), f32)],
        scratch_shapes=[pltpu.VMEM((t + 2 * HALO, XBC), f32), pltpu.VMEM((GROUPS, STATE, GROUP_W), f32),
                        pltpu.VMEM((t, D), f32), pltpu.VMEM((t, D), bf16), pltpu.VMEM((t, D), f32),
                        pltpu.VMEM((t, D), bf16)],
        compiler_params=pltpu.CompilerParams(dimension_semantics=("arbitrary", "arbitrary"),
                                             vmem_limit_bytes=VMEM_LIMIT),
        name="pass1_bwd",
    )(x, x, x, mod3, ln0g, ln0b, w_main, w_dtT, convw, convb, dtb_col, alog_col, s_b, gmg, gmb, ws, bsx, bg)


def _pass2_kernel(nt, x0a_ref, moda_ref, modb_ref, sz_ref, xs_ref, bc_ref, yb_ref, ygm_ref, gate_ref, dt_ref,
                  alog_ref, dsk_ref, ng_ref, wsp_ref, wgp_ref, wo_ref, ln1g_ref, ln1b_ref, sf_ref,
                  w1_ref, w3_ref, w2_ref, ln2g_ref, ln2b_ref,
                  o_ref, hf_ref, ys_ref, mg_ref, x1_ref, hm_ref, a_ref, ff_ref):
    t = x0a_ref.shape[0]
    nch = t // CHUNK
    step = pl.program_id(0)

    @pl.when(step == 0)
    def _():
        x1_ref[...] = jnp.zeros(x1_ref.shape, f32)

    @pl.when(step % nt == 0)
    def _():
        hf_ref[...] = sf_ref[...]

    hm_ref[...] = (x1_ref[...] * (1.0 + modb_ref[:, 4 * D:5 * D]) + modb_ref[:, 3 * D:4 * D]).astype(bf16)

    def up_task(n0):
        def run():
            cols = slice(n0, n0 + NBLK)
            a_ref[:, cols] = (_silu(_dot(hm_ref[...], w1_ref[:, cols])) * _dot(hm_ref[...], w3_ref[:, cols])).astype(bf16)
        return run

    def down_task(c0):
        def run():
            ff_ref[:, c0:c0 + NBLK] = _dot(a_ref[...], w2_ref[:, c0:c0 + NBLK])
        return run

    def gm_task(c0):
        def run():
            cols = slice(c0, c0 + NBLK)
            mg_ref[:, cols] = (gate_ref[:, D + c0:D + c0 + NBLK].astype(f32)
                               * _dot(ygm_ref[...], wgp_ref[:, cols]))
        return run

    tasks = ([gm_task(c0) for c0 in range(0, D, NBLK)] + [up_task(n0) for n0 in range(0, D_FF, NBLK)]
             + [down_task(c0) for c0 in range(0, D, NBLK)])
    per = -(-len(tasks) // nch)

    a_col = -jnp.exp(alog_ref[...])
    dt = dt_ref[...]
    preps = [_ssd_prep(dt[0:HEADS, c * CHUNK:(c + 1) * CHUNK], a_col[0:HEADS], True) for c in range(nch)]
    dsk = dsk_ref[0:1, :] + dsk_ref[1:2, :]
    for c in range(nch):
        rows = slice(c * CHUNK, (c + 1) * CHUNK)
        xs_bf = xs_ref[rows, :]
        xs_f = xs_bf.astype(f32)
        y = _ssd_chunk(preps[c], xs_bf, xs_f, bc_ref[rows, 0:GROUPS * STATE], bc_ref[rows, GROUPS * STATE:],
                       hf_ref, fwd=True, fillers=tasks[c * per:(c + 1) * per])
        y = y + yb_ref[rows, :].astype(f32) + xs_f * dsk
        hh = y * sz_ref[rows, :].astype(f32)
        hh = hh * lax.rsqrt(jnp.mean(hh * hh, axis=-1, keepdims=True) + LN_EPS) * ng_ref[...]
        ys_ref[rows, :] = hh.astype(bf16)

    o_ref[...] = _ln(ALPHA * x1_ref[...] + modb_ref[:, 5 * D:6 * D] * ff_ref[...], ln2g_ref[...], ln2b_ref[...])

    merged = gate_ref[:, 0:D].astype(f32) * _dot(ys_ref[...], wsp_ref[...]) + mg_ref[...]
    out_x = _dot(merged.astype(bf16), wo_ref[...])
    x1_ref[...] = _ln(x0a_ref[...] + moda_ref[:, 2 * D:3 * D] * out_x, ln1g_ref[...], ln1b_ref[...])


def _pass2_call(x0a, mod3, sz, xs, bc, yb, ygm, gate, dt, alog_col, dsk, ng, wsp, wgp, wo, ln1g, ln1b, s_f,
                w1, w3, w2, ln2g, ln2b, t):
    bsz, seq, _ = x0a.shape
    nt = seq // t
    last = bsz * nt - 1
    cur = lambda s: jnp.minimum(s, last)
    prev = lambda s: jnp.maximum(s - 1, 0)
    const = lambda shape: pl.BlockSpec(shape, lambda s: (0,) * len(shape), pipeline_mode=pl.Buffered(1))
    tok = lambda w: pl.BlockSpec((None, t, w), lambda s: (cur(s) // nt, cur(s) % nt, 0))
    return pl.pallas_call(
        functools.partial(_pass2_kernel, nt),
        grid=(bsz * nt + 1,),
        in_specs=[tok(D), pl.BlockSpec((None, 1, D_MOD), lambda s: (cur(s) // nt, 0, 0)),
                  pl.BlockSpec((None, 1, D_MOD), lambda s: (prev(s) // nt, 0, 0)),
                  tok(D), tok(D), tok(2 * GROUPS * STATE), tok(D), tok(D), tok(2 * D),
                  pl.BlockSpec((None, 2 * HEADS, t), lambda s: (cur(s) // nt, 0, cur(s) % nt)),
                  const((2 * HEADS, 1)), const((2, D)), const((1, D)),
                  const((D, D)), const((D, D)), const((D, D)), const((1, D)), const((1, D)),
                  pl.BlockSpec((None, GROUPS, STATE, GROUP_W), lambda s: (cur(s) // nt, 0, 0, 0)),
                  const((D, D_FF)), const((D, D_FF)), const((D_FF, D)), const((1, D)), const((1, D))],
        out_specs=pl.BlockSpec((None, t, D), lambda s: (prev(s) // nt, prev(s) % nt, 0)),
        out_shape=jax.ShapeDtypeStruct((bsz, seq, D), f32),
        scratch_shapes=[pltpu.VMEM((GROUPS, STATE, GROUP_W), f32), pltpu.VMEM((t, D), bf16),
                        pltpu.VMEM((t, D), f32), pltpu.VMEM((t, D), f32), pltpu.VMEM((t, D), bf16),
                        pltpu.VMEM((t, D_FF), bf16), pltpu.VMEM((t, D), f32)],
        compiler_params=pltpu.CompilerParams(dimension_semantics=("arbitrary",), vmem_limit_bytes=VMEM_LIMIT),
        name="pass2_ffn",
    )(x0a, mod3, mod3, sz, xs, bc, yb, ygm, gate, dt, alog_col, dsk, ng, wsp, wgp, wo, ln1g, ln1b, s_f,
      w1, w3, w2, ln2g, ln2b)


def kernel(x, c, ctx, c_ctx, ln0_g, ln0_b, w_ada, b_ada, w_in, conv_w, conv_b, dt_bias, a_log, d_skip, ssd_norm_g, gm_norm_g, gm_norm_b, w_spatial, b_spatial, b_gate, w_ssd_proj, w_gm_proj, w_out, ln1_g, ln1_b, w_ff1, w_ff3, w_ff2, ln2_g, ln2_b):
    bsz, seq, _ = x.shape
    assert x.shape[2] == D and w_in.shape == (DEPTH, D, W_MAIN + 2 * HEADS) and bsz < 16
    assert seq % 512 == 0 and ctx.shape[1] % CHUNK == 0
    row = lambda a: a.reshape(1, -1)

    w_in0 = w_in[0]
    o_dt = D + XBC
    w_main = jnp.concatenate([w_in0[:, :o_dt], w_in0[:, o_dt + 2 * HEADS:]], axis=1).astype(bf16)
    w_xbc = w_in0[:, D:o_dt].astype(bf16)
    w_dtT = w_in0[:, o_dt:o_dt + 2 * HEADS].T.astype(bf16)
    dtb_col = dt_bias[0].reshape(2 * HEADS, 1)
    alog_col = a_log[0].reshape(2 * HEADS, 1)
    dsk = jnp.repeat(d_skip[0], HEAD_DIM, axis=1)
    bsx = jnp.repeat(b_spatial[0].T, GM_GROUP_DIM, axis=1)
    ln0g, ln0b = row(ln0_g), row(ln0_b)

    cc = jnp.concatenate([c, c_ctx[None, :], jnp.zeros((15 - bsz, D), f32)], axis=0)
    mod = _mod_call(cc, w_ada[0], row(b_ada[0]))
    mod3 = mod.reshape(16, 1, D_MOD)
    s_f, s_b = _ctx_call(ctx, mod3[bsz:bsz + 1], ln0g, ln0b, w_xbc, w_dtT,
                         conv_w[0], row(conv_b[0]), dtb_col, alog_col)

    t = 256
    x0a, sz, xs, bc, yb, ygm, gate, dt = _pass1_call(
        x, mod3, ln0g, ln0b, w_main, w_dtT, conv_w[0], row(conv_b[0]), dtb_col, alog_col, s_b,
        row(gm_norm_g[0]), row(gm_norm_b[0]), w_spatial[0].astype(bf16), bsx, row(b_gate[0]), t)
    return _pass2_call(x0a, mod3, sz, xs, bc, yb, ygm, gate, dt, alog_col, dsk, row(ssd_norm_g[0]),
                       w_ssd_proj[0].astype(bf16), w_gm_proj[0].astype(bf16), w_out[0].astype(bf16),
                       row(ln1_g[0]), row(ln1_b[0]), s_f,
                       w_ff1[0].astype(bf16), w_ff3[0].astype(bf16), w_ff2[0].astype(bf16),
                       row(ln2_g[0]), row(ln2_b[0]), t)
```

```python
import functools

import jax
import jax.numpy as jnp
from jax import lax
from jax.experimental import pallas as pl
from jax.experimental.pallas import tpu as pltpu

f32 = jnp.float32
bf16 = jnp.bfloat16

D = 1024
HEADS = 16
HEAD_DIM = 64
GROUPS = 2
GROUP_W = D // GROUPS
STATE = 128
CONV = 5
CHUNK = 128
XBC = D + 2 * GROUPS * STATE
GM_GROUPS = 8
GM_GROUP_DIM = D // GM_GROUPS
D_FF = 2816
D_MOD = 6 * D
DEPTH = 1
ALPHA = (2 * DEPTH) ** 0.25
LN_EPS = 1e-5
HALO = 8
LANES = 128
NBLK = 256

D_PROJ = D + XBC + 2 * HEADS + 4 * D

VMEM_LIMIT = 56 * 1024 * 1024


def _dot(a, b):
    return jnp.dot(a, b, preferred_element_type=f32)


def _dot_nt(a, b):
    return lax.dot_general(a, b, (((1,), (1,)), ((), ())), preferred_element_type=f32)


def _dot_tn(a, b):
    return lax.dot_general(a, b, (((0,), (0,)), ((), ())), preferred_element_type=f32)


def _norm_rows(xf):
    mu = jnp.mean(xf, axis=-1, keepdims=True)
    xc = xf - mu
    return xc * lax.rsqrt(jnp.mean(xc * xc, axis=-1, keepdims=True) + LN_EPS)


def _ln(xf, g, b):
    return _norm_rows(xf) * g + b


def _sigmoid(x):
    return 0.5 * jnp.tanh(0.5 * x) + 0.5


def _silu(x):
    h = 0.5 * x
    return h * jnp.tanh(h) + h


def _softplus(x):
    return jnp.maximum(x, 0.0) + jnp.log1p(jnp.exp(-jnp.abs(x)))


def _scan_rows(a, tri):
    hi = a.astype(bf16)
    r1 = a - hi.astype(f32)
    mid = r1.astype(bf16)
    lo = (r1 - mid.astype(f32)).astype(bf16)
    cs = _dot(jnp.concatenate([hi, mid, lo], axis=0), tri)
    return cs[0:16] + cs[16:32] + cs[32:48]


def _dir_rows(dt, a_col, fwd):
    k = lax.broadcasted_iota(jnp.int32, (CHUNK, CHUNK), 0)
    j = lax.broadcasted_iota(jnp.int32, (CHUNK, CHUNK), 1)
    tri = jnp.where((k <= j) if fwd else (k >= j), 1.0, 0.0).astype(bf16)
    cum = _scan_rows(dt * a_col, tri)
    total = cum[:, CHUNK - 1:CHUNK] if fwd else cum[:, 0:1]
    return cum, jnp.exp(total - cum) * dt, jnp.exp(cum)


def _to_cols(rows):
    pad = jnp.zeros((LANES - 16 * len(rows), CHUNK), f32)
    return jnp.concatenate(list(rows) + [pad], axis=0).T


def _expand_heads(m):
    lane = lax.broadcasted_iota(jnp.int32, (CHUNK, LANES), 1)
    pieces = []
    for p in range(HEADS // 2):
        pieces.append(jnp.where(lane < HEAD_DIM, m[:, 2 * p:2 * p + 1], m[:, 2 * p + 1:2 * p + 2]))
    return jnp.concatenate(pieces, axis=1)


def _ssd_prep(dt, a_col, fwd):
    cum, wdt, ecum = _dir_rows(dt, a_col, fwd)
    return cum - jnp.log(dt), _to_cols([cum, wdt, ecum])


def _ssd_chunk(prep, xs_bf, xs_f, bm, cm, h_ref, fwd, want_y=True, fillers=()):
    cumj, cols = prep
    fillers = list(fillers)
    ecum_x = _expand_heads(cols[:, 32:48])
    y = None
    if want_y:
        ii = lax.broadcasted_iota(jnp.int32, (CHUNK, CHUNK), 0)
        jj = lax.broadcasted_iota(jnp.int32, (CHUNK, CHUNK), 1)
        keep = (ii >= jj) if fwd else (ii <= jj)
        lane = lax.broadcasted_iota(jnp.int32, (CHUNK, LANES), 1)
        cb = [_dot_nt(cm[:, g * STATE:(g + 1) * STATE], bm[:, g * STATE:(g + 1) * STATE]) for g in range(GROUPS)]
        pieces = []
        for p in range(HEADS // 2):
            g = (2 * p) // (HEADS // GROUPS)
            ws = []
            for h in (2 * p, 2 * p + 1):
                seg = cols[:, h:h + 1] - cumj[h:h + 1, :]
                ws.append(jnp.where(keep, cb[g] * jnp.exp(seg), 0.0))
            lhs = jnp.concatenate(ws, axis=1).astype(bf16)
            x2 = xs_bf[:, p * LANES:(p + 1) * LANES]
            zero = jnp.zeros_like(x2)
            rhs = jnp.concatenate([jnp.where(lane < HEAD_DIM, x2, zero),
                                   jnp.where(lane >= HEAD_DIM, x2, zero)], axis=0)
            pieces.append(_dot(lhs, rhs))
            if fillers:
                fillers.pop(0)()
        y_diag = jnp.concatenate(pieces, axis=1)
        y_off = jnp.concatenate([_dot(cm[:, g * STATE:(g + 1) * STATE], h_ref[g].astype(bf16))
                                 for g in range(GROUPS)], axis=1)
        y = y_diag + y_off * ecum_x
    xw = (xs_f * _expand_heads(cols[:, 16:32])).astype(bf16)
    dec = ecum_x[CHUNK - 1:CHUNK, :] if fwd else ecum_x[0:1, :]
    for g in range(GROUPS):
        st = _dot_tn(bm[:, g * STATE:(g + 1) * STATE], xw[:, g * GROUP_W:(g + 1) * GROUP_W])
        h_ref[g] = dec[:, g * GROUP_W:(g + 1) * GROUP_W] * h_ref[g] + st
    for f in fillers:
        f()
    return y


def _conv_silu(ext_ref, convw_ref, convb_ref, t, c0, c1):
    n = (t + 2 * HALO) // 8
    w = c1 - c0
    p = ext_ref[:, c0:c1].reshape(n, 8, w)
    sub = lax.broadcasted_iota(jnp.int32, (1, 8, w), 1)
    mid = CONV // 2
    acc = convb_ref[:, c0:c1].reshape(1, 1, w) + p[1:n - 1] * convw_ref[mid:mid + 1, c0:c1].reshape(1, 1, w)
    for k in range(CONV):
        s = k - mid
        if s > 0:
            mixed = jnp.where(sub >= s, p[1:n - 1], p[2:n])
            shifted = pltpu.roll(mixed, 8 - s, axis=1)
        elif s < 0:
            mixed = jnp.where(sub < 8 + s, p[1:n - 1], p[0:n - 2])
            shifted = pltpu.roll(mixed, -s, axis=1)
        else:
            continue
        acc = acc + shifted * convw_ref[k:k + 1, c0:c1].reshape(1, 1, w)
    return _silu(acc).reshape(t, w)


def _mod_kernel(c_ref, w_ref, b_ref, o_ref):
    s = _silu(c_ref[...]).astype(bf16)
    o_ref[...] = _dot(s, w_ref[...].astype(bf16)) + b_ref[...]


def _mod_call(cc, w_ada, b_ada):
    nb = 1024
    return pl.pallas_call(
        _mod_kernel,
        grid=(D_MOD // nb,),
        in_specs=[pl.BlockSpec((16, D), lambda n: (0, 0)),
                  pl.BlockSpec((D, nb), lambda n: (0, n)),
                  pl.BlockSpec((1, nb), lambda n: (0, n))],
        out_specs=pl.BlockSpec((16, nb), lambda n: (0, n)),
        out_shape=jax.ShapeDtypeStruct((16, D_MOD), f32),
        compiler_params=pltpu.CompilerParams(dimension_semantics=("arbitrary",), vmem_limit_bytes=VMEM_LIMIT),
        name="adaln_mod",
    )(cc, w_ada, b_ada)


def _ctx_kernel(ctx_ref, mod_ref, ln0g_ref, ln0b_ref, wzx_ref, wdtT_ref, convw_ref, convb_ref, dtb_ref,
                alog_ref, sf_ref, sb_ref, ext_ref, xs_ref, bc_ref):
    cl = ctx_ref.shape[0]
    sh1 = mod_ref[:, 0:D]
    sc1 = mod_ref[:, D:2 * D]
    xm = (_ln(ctx_ref[...], ln0g_ref[...], ln0b_ref[...]) * (1.0 + sc1) + sh1).astype(bf16)
    ext_ref[pl.ds(0, HALO), :] = jnp.zeros((HALO, XBC), f32)
    ext_ref[pl.ds(HALO + cl, HALO), :] = jnp.zeros((HALO, XBC), f32)
    ext_ref[pl.ds(HALO, cl), :] = _dot(xm, wzx_ref[:, D:])
    for c0 in range(0, D, NBLK):
        xs_ref[:, c0:c0 + NBLK] = _conv_silu(ext_ref, convw_ref, convb_ref, cl, c0, c0 + NBLK)
    bc_ref[...] = _conv_silu(ext_ref, convw_ref, convb_ref, cl, D, XBC).astype(bf16)
    dt = _softplus(_dot_nt(wdtT_ref[...], xm) + dtb_ref[...])
    a_col = -jnp.exp(alog_ref[...])
    sf_ref[...] = jnp.zeros(sf_ref.shape, f32)
    sb_ref[...] = jnp.zeros(sb_ref.shape, f32)
    nch = cl // CHUNK
    for fwd, h_ref, order in ((True, sf_ref, range(nch)), (False, sb_ref, range(nch - 1, -1, -1))):
        r0 = 0 if fwd else HEADS
        for c in order:
            rows = slice(c * CHUNK, (c + 1) * CHUNK)
            dtc = dt[r0:r0 + HEADS, rows]
            _ssd_chunk(_ssd_prep(dtc, a_col[r0:r0 + HEADS], fwd), None, xs_ref[rows, :],
                       bc_ref[rows, 0:GROUPS * STATE], None, h_ref, fwd, want_y=False)


def _ctx_call(ctx, mod3, ln0g, ln0b, w_zx, w_dtT, convw, convb, dtb_col, alog_col):
    bsz, cl, _ = ctx.shape
    full = lambda shape: pl.BlockSpec(shape, lambda b: (0,) * len(shape))
    st_shape = jax.ShapeDtypeStruct((bsz, GROUPS, STATE, GROUP_W), f32)
    st_spec = pl.BlockSpec((None, GROUPS, STATE, GROUP_W), lambda b: (b, 0, 0, 0))
    return pl.pallas_call(
        _ctx_kernel,
        grid=(bsz,),
        in_specs=[pl.BlockSpec((None, cl, D), lambda b: (b, 0, 0)),
                  pl.BlockSpec((None, 1, 2 * D), lambda b: (0, 0, 0)),
                  full((1, D)), full((1, D)), full((D, D + XBC)), full((2 * HEADS, D)),
                  full((CONV, XBC)), full((1, XBC)), full((2 * HEADS, 1)), full((2 * HEADS, 1))],
        out_specs=[st_spec, st_spec],
        out_shape=[st_shape, st_shape],
        scratch_shapes=[pltpu.VMEM((cl + 2 * HALO, XBC), f32), pltpu.VMEM((cl, D), f32),
                        pltpu.VMEM((cl, 2 * GROUPS * STATE), bf16)],
        compiler_params=pltpu.CompilerParams(dimension_semantics=("arbitrary",), vmem_limit_bytes=VMEM_LIMIT),
        name="ctx_states",
    )(ctx, mod3, ln0g, ln0b, w_zx, w_dtT, convw, convb, dtb_col, alog_col)


def _pass1_kernel(x_ref, xp_ref, xn_ref, mod_ref, ln0g_ref, ln0b_ref, wzx_ref, wuvg_ref, wdtT_ref, convw_ref,
                  convb_ref, dtb_ref, alog_ref, sb_ref,
                  x0a_ref, z_ref, xs_ref, bc_ref, yb_ref, uvg_ref, dt_ref,
                  ext_ref, hb_ref, xsf_ref):
    t = x_ref.shape[0]
    nch = t // CHUNK
    step = pl.program_id(1)
    nt = pl.num_programs(1)
    tile = nt - 1 - step

    @pl.when(step == 0)
    def _():
        hb_ref[...] = sb_ref[...]

    g0 = ln0g_ref[...]
    b0 = ln0b_ref[...]
    sc1p = 1.0 + mod_ref[:, D:2 * D]
    gm = g0 * sc1p
    bm = b0 * sc1p + mod_ref[:, 0:D]

    xn = _norm_rows(x_ref[...])
    x0a_ref[...] = xn * (ALPHA * g0) + ALPHA * b0
    xm_f = xn * gm + bm
    xm = xm_f.astype(bf16)
    ext = jnp.concatenate([_norm_rows(xp_ref[...]) * gm + bm, xm_f, _norm_rows(xn_ref[...]) * gm + bm],
                          axis=0).astype(bf16)
    row = lax.broadcasted_iota(jnp.int32, (t + 2 * HALO, 1), 0)
    valid = jnp.logical_and(jnp.logical_or(row >= HALO, tile > 0), jnp.logical_or(row < HALO + t, tile < nt - 1))

    def ext_task(c0):
        def run():
            w = XBC // 3
            ext_ref[:, c0:c0 + w] = jnp.where(valid, _dot(ext, wzx_ref[:, D + c0:D + c0 + w]), 0.0)
        return run

    def proj_task(w_ref, c0, dst_ref):
        def run():
            cols = slice(c0, c0 + NBLK)
            dst_ref[:, cols] = _dot(xm, w_ref[:, cols]).astype(bf16)
        return run

    def conv_task(c0):
        def run():
            blk = _conv_silu(ext_ref, convw_ref, convb_ref, t, c0, c0 + LANES)
            if c0 < D:
                xsf_ref[:, c0:c0 + LANES] = blk
                xs_ref[:, c0:c0 + LANES] = blk.astype(bf16)
            else:
                bc_ref[:, c0 - D:c0 - D + LANES] = blk.astype(bf16)
        return run

    z_t = [proj_task(wzx_ref, c0, z_ref) for c0 in range(0, D, NBLK)]
    p_t = [proj_task(wuvg_ref, c0, uvg_ref) for c0 in range(0, 4 * D, NBLK)]
    e_t = [ext_task(c0) for c0 in range(0, XBC, XBC // 3)]
    c_t = [conv_task(c0) for c0 in range(0, XBC, LANES)]
    p_t = z_t + p_t
    fill = (len(p_t) - len(c_t)) // nch
    assert fill * nch == len(p_t) - len(c_t)

    for task in e_t:
        task()
    dt = _softplus(_dot_nt(wdtT_ref[...], xm) + dtb_ref[...])
    dt_ref[...] = dt
    a_col = -jnp.exp(alog_ref[...])
    preps = [_ssd_prep(dt[HEADS:, c * CHUNK:(c + 1) * CHUNK], a_col[HEADS:], False) for c in range(nch)]
    for proj, conv in zip(p_t, c_t):
        proj()
        conv()
    rest = p_t[len(c_t):]
    for i, c in enumerate(range(nch - 1, -1, -1)):
        rows = slice(c * CHUNK, (c + 1) * CHUNK)
        y = _ssd_chunk(preps[c], xs_ref[rows, :], xsf_ref[rows, :], bc_ref[rows, 0:GROUPS * STATE],
                       bc_ref[rows, GROUPS * STATE:], hb_ref, fwd=False, fillers=rest[i * fill:(i + 1) * fill])
        yb_ref[rows, :] = y.astype(bf16)


def _pass1_call(x, mod3, ln0g, ln0b, w_zx, w_uvg, w_dtT, convw, convb, dtb_col, alog_col, s_b, t):
    bsz, seq, _ = x.shape
    nt = seq // t
    hb = t // HALO
    nhb = seq // HALO
    const = lambda shape: pl.BlockSpec(shape, lambda b, s: (0,) * len(shape), pipeline_mode=pl.Buffered(1))
    tok = lambda w: pl.BlockSpec((None, t, w), lambda b, s: (b, nt - 1 - s, 0))
    act = lambda w, dt_: jax.ShapeDtypeStruct((bsz, seq, w), dt_)
    return pl.pallas_call(
        _pass1_kernel,
        grid=(bsz, nt),
        in_specs=[tok(D),
                  pl.BlockSpec((None, HALO, D), lambda b, s: (b, jnp.maximum((nt - 1 - s) * hb - 1, 0), 0)),
                  pl.BlockSpec((None, HALO, D), lambda b, s: (b, jnp.minimum((nt - s) * hb, nhb - 1), 0)),
                  pl.BlockSpec((None, 1, 2 * D), lambda b, s: (b, 0, 0)),
                  const((1, D)), const((1, D)), const((D, D + XBC)), const((D, 4 * D)), const((2 * HEADS, D)),
                  const((CONV, XBC)), const((1, XBC)), const((2 * HEADS, 1)), const((2 * HEADS, 1)),
                  pl.BlockSpec((None, GROUPS, STATE, GROUP_W), lambda b, s: (b, 0, 0, 0))],
        out_specs=[tok(D), tok(D), tok(D), tok(2 * GROUPS * STATE), tok(D), tok(4 * D),
                   pl.BlockSpec((None, 2 * HEADS, t), lambda b, s: (b, 0, nt - 1 - s))],
        out_shape=[act(D, f32), act(D, bf16), act(D, bf16), act(2 * GROUPS * STATE, bf16), act(D, bf16),
                   act(4 * D, bf16), jax.ShapeDtypeStruct((bsz, 2 * HEADS, seq), f32)],
        scratch_shapes=[pltpu.VMEM((t + 2 * HALO, XBC), f32), pltpu.VMEM((GROUPS, STATE, GROUP_W), f32),
                        pltpu.VMEM((t, D), f32)],
        compiler_params=pltpu.CompilerParams(dimension_semantics=("arbitrary", "arbitrary"),
                                             vmem_limit_bytes=VMEM_LIMIT),
        name="pass1_bwd",
    )(x, x, x, mod3, ln0g, ln0b, w_zx, w_uvg, w_dtT, convw, convb, dtb_col, alog_col, s_b)


def _pass2_kernel(nt, x0a_ref, moda_ref, modb_ref, z_ref, xs_ref, bc_ref, yb_ref, uvg_ref, dt_ref,
                  alog_ref, dsk_ref, ng_ref, gmg_ref, gmb_ref, ws_ref, bsx_ref, bg_ref,
                  wsp_ref, wgp_ref, wo_ref, ln1g_ref, ln1b_ref, sf_ref,
                  w1_ref, w3_ref, w2_ref, ln2g_ref, ln2b_ref,
                  o_ref, hf_ref, ys_ref, mg_ref, x1_ref, hm_ref, a_ref, ff_ref, vn_ref, ygm_ref):
    t = x0a_ref.shape[0]
    nch = t // CHUNK
    step = pl.program_id(0)

    @pl.when(step == 0)
    def _():
        x1_ref[...] = jnp.zeros(x1_ref.shape, f32)

    @pl.when(step % nt == 0)
    def _():
        hf_ref[...] = sf_ref[...]

    hm_ref[...] = (x1_ref[...] * (1.0 + modb_ref[:, 4 * D:5 * D]) + modb_ref[:, 3 * D:4 * D]).astype(bf16)

    def up_task(n0):
        def run():
            cols = slice(n0, n0 + NBLK)
            a_ref[:, cols] = (_silu(_dot(hm_ref[...], w1_ref[:, cols])) * _dot(hm_ref[...], w3_ref[:, cols])).astype(bf16)
        return run

    wide = 2 * NBLK

    def down_task(c0):
        def run():
            ff_ref[:, c0:c0 + wide] = _dot(a_ref[...], w2_ref[:, c0:c0 + wide])
        return run

    def vnorm_task(c):
        def run():
            rows = slice(c * CHUNK, (c + 1) * CHUNK)
            vn_ref[rows, :] = _ln(jax.nn.gelu(uvg_ref[rows, D:2 * D].astype(f32)),
                                  gmg_ref[...], gmb_ref[...]).astype(bf16)
        return run

    def gating_task(g):
        def run():
            cols = slice(g * GM_GROUP_DIM, (g + 1) * GM_GROUP_DIM)
            rhs = jnp.concatenate([vn_ref[c * CHUNK:(c + 1) * CHUNK, cols] for c in range(nch)], axis=1)
            mixed = _dot(ws_ref[g], rhs)
            for c in range(nch):
                rows = slice(c * CHUNK, (c + 1) * CHUNK)
                ygm_ref[rows, cols] = (jax.nn.gelu(uvg_ref[rows, cols].astype(f32))
                                       * (mixed[:, c * GM_GROUP_DIM:(c + 1) * GM_GROUP_DIM]
                                          + bsx_ref[:, cols])).astype(bf16)
        return run

    def gate(c0, c1):
        return _sigmoid(uvg_ref[:, 2 * D + c0:2 * D + c1].astype(f32) + bg_ref[:, c0:c1])

    def gm_task(c0):
        def run():
            cols = slice(c0, c0 + wide)
            mg_ref[:, cols] = gate(D + c0, D + c0 + wide) * _dot(ygm_ref[...], wgp_ref[:, cols])
        return run

    up_t = [up_task(n0) for n0 in range(0, D_FF, NBLK)]
    down_t = [down_task(c0) for c0 in range(0, D, wide)]
    gm_t = [gm_task(c0) for c0 in range(0, D, wide)]
    s_t = [gating_task(g) for g in range(GM_GROUPS)]
    assert (nch, len(up_t), len(gm_t)) == (2, 11, 2)

    for task in (up_t[0], vnorm_task(0), up_t[1], vnorm_task(1), up_t[2], *s_t[0:4], up_t[3], *s_t[4:8]):
        task()
    fillers = [[gm_t[0], up_t[4], up_t[5], gm_t[1], up_t[6], up_t[7]],
               [up_t[8], up_t[9], up_t[10], *down_t]]

    a_col = -jnp.exp(alog_ref[...])
    dt = dt_ref[...]
    preps = [_ssd_prep(dt[0:HEADS, c * CHUNK:(c + 1) * CHUNK], a_col[0:HEADS], True) for c in range(nch)]
    dsk = dsk_ref[0:1, :] + dsk_ref[1:2, :]
    for c in range(nch):
        rows = slice(c * CHUNK, (c + 1) * CHUNK)
        xs_bf = xs_ref[rows, :]
        xs_f = xs_bf.astype(f32)
        y = _ssd_chunk(preps[c], xs_bf, xs_f, bc_ref[rows, 0:GROUPS * STATE], bc_ref[rows, GROUPS * STATE:],
                       hf_ref, fwd=True, fillers=fillers[c])
        y = y + yb_ref[rows, :].astype(f32) + xs_f * dsk
        hh = y * _silu(z_ref[rows, :].astype(f32))
        hh = hh * lax.rsqrt(jnp.mean(hh * hh, axis=-1, keepdims=True) + LN_EPS) * ng_ref[...]
        ys_ref[rows, :] = hh.astype(bf16)

    o_ref[...] = _ln(ALPHA * x1_ref[...] + modb_ref[:, 5 * D:6 * D] * ff_ref[...], ln2g_ref[...], ln2b_ref[...])

    merged = gate(0, D) * _dot(ys_ref[...], wsp_ref[...]) + mg_ref[...]
    out_x = _dot(merged.astype(bf16), wo_ref[...])
    x1_ref[...] = _ln(x0a_ref[...] + moda_ref[:, 2 * D:3 * D] * out_x, ln1g_ref[...], ln1b_ref[...])


def _pass2_call(x0a, mod3, z, xs, bc, yb, uvg, dt, alog_col, dsk, ng, gmg, gmb, ws, bsx, bg, wsp, wgp, wo,
                ln1g, ln1b, s_f, w1, w3, w2, ln2g, ln2b, t):
    bsz, seq, _ = x0a.shape
    nt = seq // t
    last = bsz * nt - 1
    cur = lambda s: jnp.minimum(s, last)
    prev = lambda s: jnp.maximum(s - 1, 0)
    const = lambda shape: pl.BlockSpec(shape, lambda s: (0,) * len(shape), pipeline_mode=pl.Buffered(1))
    tok = lambda w: pl.BlockSpec((None, t, w), lambda s: (cur(s) // nt, cur(s) % nt, 0))
    return pl.pallas_call(
        functools.partial(_pass2_kernel, nt),
        grid=(bsz * nt + 1,),
        in_specs=[tok(D), pl.BlockSpec((None, 1, D_MOD), lambda s: (cur(s) // nt, 0, 0)),
                  pl.BlockSpec((None, 1, D_MOD), lambda s: (prev(s) // nt, 0, 0)),
                  tok(D), tok(D), tok(2 * GROUPS * STATE), tok(D), tok(4 * D),
                  pl.BlockSpec((None, 2 * HEADS, t), lambda s: (cur(s) // nt, 0, cur(s) % nt)),
                  const((2 * HEADS, 1)), const((2, D)), const((1, D)),
                  const((1, D)), const((1, D)), const((GM_GROUPS, CHUNK, CHUNK)), const((CHUNK, D)),
                  const((1, 2 * D)),
                  const((D, D)), const((D, D)), const((D, D)), const((1, D)), const((1, D)),
                  pl.BlockSpec((None, GROUPS, STATE, GROUP_W), lambda s: (cur(s) // nt, 0, 0, 0)),
                  const((D, D_FF)), const((D, D_FF)), const((D_FF, D)), const((1, D)), const((1, D))],
        out_specs=pl.BlockSpec((None, t, D), lambda s: (prev(s) // nt, prev(s) % nt, 0)),
        out_shape=jax.ShapeDtypeStruct((bsz, seq, D), f32),
        scratch_shapes=[pltpu.VMEM((GROUPS, STATE, GROUP_W), f32), pltpu.VMEM((t, D), bf16),
                        pltpu.VMEM((t, D), f32), pltpu.VMEM((t, D), f32), pltpu.VMEM((t, D), bf16),
                        pltpu.VMEM((t, D_FF), bf16), pltpu.VMEM((t, D), f32), pltpu.VMEM((t, D), bf16),
                        pltpu.VMEM((t, D), bf16)],
        compiler_params=pltpu.CompilerParams(dimension_semantics=("arbitrary",), vmem_limit_bytes=VMEM_LIMIT),
        name="pass2_ffn",
    )(x0a, mod3, mod3, z, xs, bc, yb, uvg, dt, alog_col, dsk, ng, gmg, gmb, ws, bsx, bg, wsp, wgp, wo,
      ln1g, ln1b, s_f, w1, w3, w2, ln2g, ln2b)


def kernel(x, c, ctx, c_ctx, ln0_g, ln0_b, w_ada, b_ada, w_in, conv_w, conv_b, dt_bias, a_log, d_skip, ssd_norm_g, gm_norm_g, gm_norm_b, w_spatial, b_spatial, b_gate, w_ssd_proj, w_gm_proj, w_out, ln1_g, ln1_b, w_ff1, w_ff3, w_ff2, ln2_g, ln2_b):
    bsz, seq, _ = x.shape
    assert x.shape[2] == D and w_in.shape == (DEPTH, D, D_PROJ) and bsz < 16
    assert seq % 512 == 0 and ctx.shape[1] % CHUNK == 0
    row = lambda a: a.reshape(1, -1)

    w_in0 = w_in[0]
    o_dt = D + XBC
    w_zx = w_in0[:, :o_dt].astype(bf16)
    w_uvg = w_in0[:, o_dt + 2 * HEADS:].astype(bf16)
    w_dtT = w_in0[:, o_dt:o_dt + 2 * HEADS].T.astype(bf16)
    dtb_col = dt_bias[0].reshape(2 * HEADS, 1)
    alog_col = a_log[0].reshape(2 * HEADS, 1)
    dsk = jnp.repeat(d_skip[0], HEAD_DIM, axis=1)
    bsx = jnp.repeat(b_spatial[0].T, GM_GROUP_DIM, axis=1)
    ln0g, ln0b = row(ln0_g), row(ln0_b)

    cc = jnp.concatenate([c, c_ctx[None, :], jnp.zeros((15 - bsz, D), f32)], axis=0)
    mod = _mod_call(cc, w_ada[0], row(b_ada[0]))
    mod3 = mod.reshape(16, 1, D_MOD)
    s_f, s_b = _ctx_call(ctx, mod3[bsz:bsz + 1], ln0g, ln0b, w_zx, w_dtT,
                         conv_w[0], row(conv_b[0]), dtb_col, alog_col)

    t = 256
    x0a, z, xs, bc, yb, uvg, dt = _pass1_call(
        x, mod3, ln0g, ln0b, w_zx, w_uvg, w_dtT, conv_w[0], row(conv_b[0]), dtb_col, alog_col, s_b, 2 * t)
    return _pass2_call(x0a, mod3, z, xs, bc, yb, uvg, dt, alog_col, dsk, row(ssd_norm_g[0]),
                       row(gm_norm_g[0]), row(gm_norm_b[0]), w_spatial[0].astype(bf16), bsx, row(b_gate[0]),
                       w_ssd_proj[0].astype(bf16), w_gm_proj[0].astype(bf16), w_out[0].astype(bf16),
                       row(ln1_g[0]), row(ln1_b[0]), s_f,
                       w_ff1[0].astype(bf16), w_ff3[0].astype(bf16), w_ff2[0].astype(bf16),
                       row(ln2_g[0]), row(ln2_b[0]), t)
```

```python
import functools

import jax
import jax.numpy as jnp
from jax import lax
from jax.experimental import pallas as pl
from jax.experimental.pallas import tpu as pltpu

f32 = jnp.float32
bf16 = jnp.bfloat16

D = 1024
HEADS = 16
HEAD_DIM = 64
GROUPS = 2
GROUP_W = D // GROUPS
STATE = 128
CONV = 5
CHUNK = 128
XBC = D + 2 * GROUPS * STATE
GM_GROUPS = 8
GM_GROUP_DIM = D // GM_GROUPS
D_FF = 2816
D_MOD = 6 * D
DEPTH = 1
ALPHA = (2 * DEPTH) ** 0.25
LN_EPS = 1e-5
HALO = 8
LANES = 128
NBLK = 256

D_PROJ = D + XBC + 2 * HEADS + 4 * D

VMEM_LIMIT = 56 * 1024 * 1024


def _dot(a, b):
    return jnp.dot(a, b, preferred_element_type=f32)


def _dot_nt(a, b):
    return lax.dot_general(a, b, (((1,), (1,)), ((), ())), preferred_element_type=f32)


def _dot_tn(a, b):
    return lax.dot_general(a, b, (((0,), (0,)), ((), ())), preferred_element_type=f32)


def _norm_rows(xf):
    mu = jnp.mean(xf, axis=-1, keepdims=True)
    xc = xf - mu
    return xc * lax.rsqrt(jnp.mean(xc * xc, axis=-1, keepdims=True) + LN_EPS)


def _ln(xf, g, b):
    return _norm_rows(xf) * g + b


def _sigmoid(x):
    return 0.5 * jnp.tanh(0.5 * x) + 0.5


def _silu(x):
    h = 0.5 * x
    return h * jnp.tanh(h) + h


def _softplus(x):
    return jnp.maximum(x, 0.0) + jnp.log1p(jnp.exp(-jnp.abs(x)))


def _scan_rows(a, tri):
    hi = a.astype(bf16)
    r1 = a - hi.astype(f32)
    mid = r1.astype(bf16)
    lo = (r1 - mid.astype(f32)).astype(bf16)
    cs = _dot(jnp.concatenate([hi, mid, lo], axis=0), tri)
    return cs[0:16] + cs[16:32] + cs[32:48]


def _dir_rows(dt, a_col, fwd):
    k = lax.broadcasted_iota(jnp.int32, (CHUNK, CHUNK), 0)
    j = lax.broadcasted_iota(jnp.int32, (CHUNK, CHUNK), 1)
    tri = jnp.where((k <= j) if fwd else (k >= j), 1.0, 0.0).astype(bf16)
    cum = _scan_rows(dt * a_col, tri)
    total = cum[:, CHUNK - 1:CHUNK] if fwd else cum[:, 0:1]
    return cum, jnp.exp(total - cum) * dt, jnp.exp(cum)


def _to_cols(rows):
    pad = jnp.zeros((LANES - 16 * len(rows), CHUNK), f32)
    return jnp.concatenate(list(rows) + [pad], axis=0).T


def _expand_heads(m):
    lane = lax.broadcasted_iota(jnp.int32, (CHUNK, LANES), 1)
    pieces = []
    for p in range(HEADS // 2):
        pieces.append(jnp.where(lane < HEAD_DIM, m[:, 2 * p:2 * p + 1], m[:, 2 * p + 1:2 * p + 2]))
    return jnp.concatenate(pieces, axis=1)


def _ssd_prep(dt, a_col, fwd):
    cum, wdt, ecum = _dir_rows(dt, a_col, fwd)
    return cum - jnp.log(dt), _to_cols([cum, wdt, ecum])


def _ssd_chunk(prep, xs_bf, xs_f, bm, cm, h_ref, fwd, want_y=True, fillers=()):
    cumj, cols = prep
    fillers = list(fillers)
    ecum_x = _expand_heads(cols[:, 32:48])
    y = None
    if want_y:
        ii = lax.broadcasted_iota(jnp.int32, (CHUNK, CHUNK), 0)
        jj = lax.broadcasted_iota(jnp.int32, (CHUNK, CHUNK), 1)
        keep = (ii >= jj) if fwd else (ii <= jj)
        lane = lax.broadcasted_iota(jnp.int32, (CHUNK, LANES), 1)
        cb = [_dot_nt(cm[:, g * STATE:(g + 1) * STATE], bm[:, g * STATE:(g + 1) * STATE]) for g in range(GROUPS)]
        pieces = []
        for p in range(HEADS // 2):
            g = (2 * p) // (HEADS // GROUPS)
            ws = []
            for h in (2 * p, 2 * p + 1):
                seg = cols[:, h:h + 1] - cumj[h:h + 1, :]
                ws.append(jnp.where(keep, cb[g] * jnp.exp(seg), 0.0))
            lhs = jnp.concatenate(ws, axis=1).astype(bf16)
            x2 = xs_bf[:, p * LANES:(p + 1) * LANES]
            zero = jnp.zeros_like(x2)
            rhs = jnp.concatenate([jnp.where(lane < HEAD_DIM, x2, zero),
                                   jnp.where(lane >= HEAD_DIM, x2, zero)], axis=0)
            pieces.append(_dot(lhs, rhs))
            if fillers:
                fillers.pop(0)()
        y_diag = jnp.concatenate(pieces, axis=1)
        y_off = jnp.concatenate([_dot(cm[:, g * STATE:(g + 1) * STATE], h_ref[g].astype(bf16))
                                 for g in range(GROUPS)], axis=1)
        y = y_diag + y_off * ecum_x
    xw = (xs_f * _expand_heads(cols[:, 16:32])).astype(bf16)
    dec = ecum_x[CHUNK - 1:CHUNK, :] if fwd else ecum_x[0:1, :]
    for g in range(GROUPS):
        st = _dot_tn(bm[:, g * STATE:(g + 1) * STATE], xw[:, g * GROUP_W:(g + 1) * GROUP_W])
        h_ref[g] = dec[:, g * GROUP_W:(g + 1) * GROUP_W] * h_ref[g] + st
    for f in fillers:
        f()
    return y


def _conv_silu(ext_ref, convw_ref, convb_ref, t, c0, c1):
    n = (t + 2 * HALO) // 8
    w = c1 - c0
    p = ext_ref[:, c0:c1].reshape(n, 8, w)
    sub = lax.broadcasted_iota(jnp.int32, (1, 8, w), 1)
    mid = CONV // 2
    acc = convb_ref[:, c0:c1].reshape(1, 1, w) + p[1:n - 1] * convw_ref[mid:mid + 1, c0:c1].reshape(1, 1, w)
    for k in range(CONV):
        s = k - mid
        if s > 0:
            mixed = jnp.where(sub >= s, p[1:n - 1], p[2:n])
            shifted = pltpu.roll(mixed, 8 - s, axis=1)
        elif s < 0:
            mixed = jnp.where(sub < 8 + s, p[1:n - 1], p[0:n - 2])
            shifted = pltpu.roll(mixed, -s, axis=1)
        else:
            continue
        acc = acc + shifted * convw_ref[k:k + 1, c0:c1].reshape(1, 1, w)
    return _silu(acc).reshape(t, w)


def _mod_kernel(c_ref, w_ref, b_ref, o_ref):
    s = _silu(c_ref[...]).astype(bf16)
    o_ref[...] = _dot(s, w_ref[...].astype(bf16)) + b_ref[...]


def _mod_call(cc, w_ada, b_ada):
    nb = 1024
    return pl.pallas_call(
        _mod_kernel,
        grid=(D_MOD // nb,),
        in_specs=[pl.BlockSpec((16, D), lambda n: (0, 0)),
                  pl.BlockSpec((D, nb), lambda n: (0, n)),
                  pl.BlockSpec((1, nb), lambda n: (0, n))],
        out_specs=pl.BlockSpec((16, nb), lambda n: (0, n)),
        out_shape=jax.ShapeDtypeStruct((16, D_MOD), f32),
        compiler_params=pltpu.CompilerParams(dimension_semantics=("arbitrary",), vmem_limit_bytes=VMEM_LIMIT),
        name="adaln_mod",
    )(cc, w_ada, b_ada)


def _ctx_kernel(ctx_ref, mod_ref, ln0g_ref, ln0b_ref, wzx_ref, wdtT_ref, convw_ref, convb_ref, dtb_ref,
                alog_ref, sf_ref, sb_ref, ext_ref, xs_ref, bc_ref):
    cl = ctx_ref.shape[0]
    sh1 = mod_ref[:, 0:D]
    sc1 = mod_ref[:, D:2 * D]
    xm = (_ln(ctx_ref[...], ln0g_ref[...], ln0b_ref[...]) * (1.0 + sc1) + sh1).astype(bf16)
    ext_ref[pl.ds(0, HALO), :] = jnp.zeros((HALO, XBC), f32)
    ext_ref[pl.ds(HALO + cl, HALO), :] = jnp.zeros((HALO, XBC), f32)
    ext_ref[pl.ds(HALO, cl), :] = _dot(xm, wzx_ref[:, D:])
    for c0 in range(0, D, NBLK):
        xs_ref[:, c0:c0 + NBLK] = _conv_silu(ext_ref, convw_ref, convb_ref, cl, c0, c0 + NBLK)
    bc_ref[...] = _conv_silu(ext_ref, convw_ref, convb_ref, cl, D, XBC).astype(bf16)
    dt = _softplus(_dot_nt(wdtT_ref[...], xm) + dtb_ref[...])
    a_col = -jnp.exp(alog_ref[...])
    sf_ref[...] = jnp.zeros(sf_ref.shape, f32)
    sb_ref[...] = jnp.zeros(sb_ref.shape, f32)
    nch = cl // CHUNK
    jobs = [(fwd, h_ref, c, 0 if fwd else HEADS)
            for k in range(nch) for fwd, h_ref, c in ((True, sf_ref, k), (False, sb_ref, nch - 1 - k))]
    preps = [_ssd_prep(dt[r0:r0 + HEADS, c * CHUNK:(c + 1) * CHUNK], a_col[r0:r0 + HEADS], fwd)
             for fwd, _, c, r0 in jobs]
    for prep, (fwd, h_ref, c, _) in zip(preps, jobs):
        rows = slice(c * CHUNK, (c + 1) * CHUNK)
        _ssd_chunk(prep, None, xs_ref[rows, :], bc_ref[rows, 0:GROUPS * STATE], None, h_ref, fwd, want_y=False)


def _ctx_call(ctx, mod3, ln0g, ln0b, w_zx, w_dtT, convw, convb, dtb_col, alog_col):
    bsz, cl, _ = ctx.shape
    full = lambda shape: pl.BlockSpec(shape, lambda b: (0,) * len(shape))
    st_shape = jax.ShapeDtypeStruct((bsz, GROUPS, STATE, GROUP_W), f32)
    st_spec = pl.BlockSpec((None, GROUPS, STATE, GROUP_W), lambda b: (b, 0, 0, 0))
    return pl.pallas_call(
        _ctx_kernel,
        grid=(bsz,),
        in_specs=[pl.BlockSpec((None, cl, D), lambda b: (b, 0, 0)),
                  pl.BlockSpec((None, 1, 2 * D), lambda b: (0, 0, 0)),
                  full((1, D)), full((1, D)), full((D, D + XBC)), full((2 * HEADS, D)),
                  full((CONV, XBC)), full((1, XBC)), full((2 * HEADS, 1)), full((2 * HEADS, 1))],
        out_specs=[st_spec, st_spec],
        out_shape=[st_shape, st_shape],
        scratch_shapes=[pltpu.VMEM((cl + 2 * HALO, XBC), f32), pltpu.VMEM((cl, D), f32),
                        pltpu.VMEM((cl, 2 * GROUPS * STATE), bf16)],
        compiler_params=pltpu.CompilerParams(dimension_semantics=("arbitrary",), vmem_limit_bytes=VMEM_LIMIT),
        name="ctx_states",
    )(ctx, mod3, ln0g, ln0b, w_zx, w_dtT, convw, convb, dtb_col, alog_col)


def _pass1_kernel(x_ref, xp_ref, xn_ref, mod_ref, ln0g_ref, ln0b_ref, wzx_ref, wuvg_ref, wdtT_ref, convw_ref,
                  convb_ref, dtb_ref, alog_ref, sb_ref, gmg_ref, gmb_ref, ws_ref, bsx_ref, bg_ref, wgp_ref,
                  x0a_ref, sz_ref, xs_ref, bc_ref, yb_ref, mg_ref, gate_ref, dt_ref,
                  ext_ref, hb_ref, xsf_ref, ug_ref, vg_ref, vn_ref, ygm_ref, gg_ref):
    t = x_ref.shape[0]
    nch = t // CHUNK
    step = pl.program_id(1)
    nt = pl.num_programs(1)
    tile = nt - 1 - step

    @pl.when(step == 0)
    def _():
        hb_ref[...] = sb_ref[...]

    g0 = ln0g_ref[...]
    b0 = ln0b_ref[...]
    sc1p = 1.0 + mod_ref[:, D:2 * D]
    gm = g0 * sc1p
    bm = b0 * sc1p + mod_ref[:, 0:D]

    xn = _norm_rows(x_ref[...])
    x0a_ref[...] = xn * (ALPHA * g0) + ALPHA * b0
    xm_f = xn * gm + bm
    xm = xm_f.astype(bf16)
    ext = jnp.concatenate([_norm_rows(xp_ref[...]) * gm + bm, xm_f, _norm_rows(xn_ref[...]) * gm + bm],
                          axis=0).astype(bf16)
    row = lax.broadcasted_iota(jnp.int32, (t + 2 * HALO, 1), 0)
    valid = jnp.logical_and(jnp.logical_or(row >= HALO, tile > 0), jnp.logical_or(row < HALO + t, tile < nt - 1))

    def ext_task(c0):
        def run():
            w = XBC // 3
            ext_ref[:, c0:c0 + w] = jnp.where(valid, _dot(ext, wzx_ref[:, D + c0:D + c0 + w]), 0.0)
        return run

    def proj_task(w_ref, w0, c0, dst_ref, fn):
        def run():
            cols = slice(c0, c0 + NBLK)
            dst_ref[:, cols] = fn(_dot(xm, w_ref[:, w0 + c0:w0 + c0 + NBLK]), cols).astype(dst_ref.dtype)
        return run

    def conv_task(c0):
        def run():
            blk = _conv_silu(ext_ref, convw_ref, convb_ref, t, c0, c0 + LANES)
            if c0 < D:
                xsf_ref[:, c0:c0 + LANES] = blk
                xs_ref[:, c0:c0 + LANES] = blk.astype(bf16)
            else:
                bc_ref[:, c0 - D:c0 - D + LANES] = blk.astype(bf16)
        return run

    def vnorm_task(c):
        def run():
            rows = slice(c * CHUNK, (c + 1) * CHUNK)
            vn_ref[rows, :] = _ln(vg_ref[rows, :], gmg_ref[...], gmb_ref[...]).astype(bf16)
        return run

    def gating_task(g):
        def run():
            cols = slice(g * GM_GROUP_DIM, (g + 1) * GM_GROUP_DIM)
            rhs = jnp.concatenate([vn_ref[c * CHUNK:(c + 1) * CHUNK, cols] for c in range(nch)], axis=1)
            mixed = _dot(ws_ref[g], rhs)
            for c in range(nch):
                rows = slice(c * CHUNK, (c + 1) * CHUNK)
                ygm_ref[rows, cols] = (ug_ref[rows, cols].astype(f32)
                                       * (mixed[:, c * GM_GROUP_DIM:(c + 1) * GM_GROUP_DIM]
                                          + bsx_ref[:, cols])).astype(bf16)
        return run

    blocks = range(0, D, NBLK)
    gelu = lambda r, cols: jax.nn.gelu(r)
    u_t = [proj_task(wuvg_ref, 0, c0, ug_ref, gelu) for c0 in blocks]
    v_t = [proj_task(wuvg_ref, D, c0, vg_ref, gelu) for c0 in blocks]
    z_t = [proj_task(wzx_ref, 0, c0, sz_ref, lambda r, cols: _silu(r)) for c0 in blocks]
    gs_t = [proj_task(wuvg_ref, 2 * D, c0, gate_ref, lambda r, cols: _sigmoid(r + bg_ref[:, cols]))
            for c0 in blocks]
    gg_t = [proj_task(wuvg_ref, 3 * D, c0, gg_ref,
                      lambda r, cols: _sigmoid(r + bg_ref[:, D + cols.start:D + cols.stop])) for c0 in blocks]

    def gm_task(c0):
        def run():
            cols = slice(c0, c0 + NBLK)
            mg_ref[:, cols] = (gg_ref[:, cols] * _dot(ygm_ref[...], wgp_ref[:, cols])).astype(bf16)
        return run

    gm_t = [gm_task(c0) for c0 in blocks]
    e_t = [ext_task(c0) for c0 in range(0, XBC, XBC // 3)]
    c_t = [conv_task(c0) for c0 in range(0, XBC, LANES)]
    n_t = [vnorm_task(c) for c in range(nch)]
    s_t = [gating_task(g) for g in range(GM_GROUPS)]
    g_t = gg_t + [t_ for pair in zip(gs_t, gm_t) for t_ in pair]
    fill = (len(g_t) - len(z_t)) // nch
    assert (len(v_t), len(c_t), len(s_t)) == (4, 12, 8) and fill * nch == len(g_t) - len(z_t)

    for task in (v_t[0], e_t[0], v_t[1], e_t[1], v_t[2], e_t[2], v_t[3]):
        task()
    dt = _softplus(_dot_nt(wdtT_ref[...], xm) + dtb_ref[...])
    dt_ref[...] = dt
    a_col = -jnp.exp(alog_ref[...])
    preps = [_ssd_prep(dt[HEADS:, c * CHUNK:(c + 1) * CHUNK], a_col[HEADS:], False) for c in range(nch)]
    assert nch == 2
    for task in (u_t[0], c_t[0], u_t[1], c_t[1], n_t[0], u_t[2], c_t[2], n_t[1], u_t[3], c_t[3]):
        task()
    for i in range(4):
        for task in (z_t[i], c_t[4 + 2 * i], g_t[i], c_t[5 + 2 * i], s_t[2 * i], s_t[2 * i + 1]):
            task()
    rest = g_t[len(z_t):]
    for i, c in enumerate(range(nch - 1, -1, -1)):
        rows = slice(c * CHUNK, (c + 1) * CHUNK)
        y = _ssd_chunk(preps[c], xs_ref[rows, :], xsf_ref[rows, :], bc_ref[rows, 0:GROUPS * STATE],
                       bc_ref[rows, GROUPS * STATE:], hb_ref, fwd=False, fillers=rest[i * fill:(i + 1) * fill])
        yb_ref[rows, :] = y.astype(bf16)


def _pass1_call(x, mod3, ln0g, ln0b, w_zx, w_uvg, w_dtT, convw, convb, dtb_col, alog_col, s_b, gmg, gmb, ws, bsx,
                bg, wgp, t):
    bsz, seq, _ = x.shape
    nt = seq // t
    hb = t // HALO
    nhb = seq // HALO
    const = lambda shape: pl.BlockSpec(shape, lambda b, s: (0,) * len(shape), pipeline_mode=pl.Buffered(1))
    tok = lambda w: pl.BlockSpec((None, t, w), lambda b, s: (b, nt - 1 - s, 0))
    act = lambda w, dt_: jax.ShapeDtypeStruct((bsz, seq, w), dt_)
    return pl.pallas_call(
        _pass1_kernel,
        grid=(bsz, nt),
        in_specs=[tok(D),
                  pl.BlockSpec((None, HALO, D), lambda b, s: (b, jnp.maximum((nt - 1 - s) * hb - 1, 0), 0)),
                  pl.BlockSpec((None, HALO, D), lambda b, s: (b, jnp.minimum((nt - s) * hb, nhb - 1), 0)),
                  pl.BlockSpec((None, 1, 2 * D), lambda b, s: (b, 0, 0)),
                  const((1, D)), const((1, D)), const((D, D + XBC)), const((D, 4 * D)), const((2 * HEADS, D)),
                  const((CONV, XBC)), const((1, XBC)), const((2 * HEADS, 1)), const((2 * HEADS, 1)),
                  pl.BlockSpec((None, GROUPS, STATE, GROUP_W), lambda b, s: (b, 0, 0, 0)),
                  const((1, D)), const((1, D)), const((GM_GROUPS, CHUNK, CHUNK)), const((CHUNK, D)),
                  const((1, 2 * D)), const((D, D))],
        out_specs=[tok(D), tok(D), tok(D), tok(2 * GROUPS * STATE), tok(D), tok(D), tok(D),
                   pl.BlockSpec((None, 2 * HEADS, t), lambda b, s: (b, 0, nt - 1 - s))],
        out_shape=[act(D, f32), act(D, bf16), act(D, bf16), act(2 * GROUPS * STATE, bf16), act(D, bf16),
                   act(D, bf16), act(D, bf16), jax.ShapeDtypeStruct((bsz, 2 * HEADS, seq), f32)],
        scratch_shapes=[pltpu.VMEM((t + 2 * HALO, XBC), f32), pltpu.VMEM((GROUPS, STATE, GROUP_W), f32),
                        pltpu.VMEM((t, D), f32), pltpu.VMEM((t, D), bf16), pltpu.VMEM((t, D), f32),
                        pltpu.VMEM((t, D), bf16), pltpu.VMEM((t, D), bf16), pltpu.VMEM((t, D), f32)],
        compiler_params=pltpu.CompilerParams(dimension_semantics=("arbitrary", "arbitrary"),
                                             vmem_limit_bytes=VMEM_LIMIT),
        name="pass1_bwd",
    )(x, x, x, mod3, ln0g, ln0b, w_zx, w_uvg, w_dtT, convw, convb, dtb_col, alog_col, s_b, gmg, gmb, ws, bsx, bg,
      wgp)


def _pass2_kernel(nt, x0a_ref, moda_ref, modb_ref, sz_ref, xs_ref, bc_ref, yb_ref, mg_ref, gate_ref, dt_ref,
                  alog_ref, dsk_ref, ng_ref, wsp_ref, wo_ref, ln1g_ref, ln1b_ref, sf_ref,
                  w1_ref, w3_ref, w2_ref, ln2g_ref, ln2b_ref,
                  o_ref, hf_ref, ys_ref, x1_ref, hm_ref, a_ref, ff_ref):
    t = x0a_ref.shape[0]
    nch = t // CHUNK
    step = pl.program_id(0)

    @pl.when(step == 0)
    def _():
        x1_ref[...] = jnp.zeros(x1_ref.shape, f32)

    @pl.when(step % nt == 0)
    def _():
        hf_ref[...] = sf_ref[...]

    hm_ref[...] = (x1_ref[...] * (1.0 + modb_ref[:, 4 * D:5 * D]) + modb_ref[:, 3 * D:4 * D]).astype(bf16)

    def up_task(n0):
        def run():
            cols = slice(n0, n0 + NBLK)
            a_ref[:, cols] = (_silu(_dot(hm_ref[...], w1_ref[:, cols])) * _dot(hm_ref[...], w3_ref[:, cols])).astype(bf16)
        return run

    def down_task(c0):
        def run():
            ff_ref[:, c0:c0 + NBLK] = _dot(a_ref[...], w2_ref[:, c0:c0 + NBLK])
        return run

    tasks = [up_task(n0) for n0 in range(0, D_FF, NBLK)] + [down_task(c0) for c0 in range(0, D, NBLK)]
    per = -(-len(tasks) // nch)

    a_col = -jnp.exp(alog_ref[...])
    dt = dt_ref[...]
    preps = [_ssd_prep(dt[0:HEADS, c * CHUNK:(c + 1) * CHUNK], a_col[0:HEADS], True) for c in range(nch)]
    dsk = dsk_ref[0:1, :] + dsk_ref[1:2, :]
    for c in range(nch):
        rows = slice(c * CHUNK, (c + 1) * CHUNK)
        xs_bf = xs_ref[rows, :]
        xs_f = xs_bf.astype(f32)
        y = _ssd_chunk(preps[c], xs_bf, xs_f, bc_ref[rows, 0:GROUPS * STATE], bc_ref[rows, GROUPS * STATE:],
                       hf_ref, fwd=True, fillers=tasks[c * per:(c + 1) * per])
        y = y + yb_ref[rows, :].astype(f32) + xs_f * dsk
        hh = y * sz_ref[rows, :].astype(f32)
        hh = hh * lax.rsqrt(jnp.mean(hh * hh, axis=-1, keepdims=True) + LN_EPS) * ng_ref[...]
        ys_ref[rows, :] = hh.astype(bf16)

    o_ref[...] = _ln(ALPHA * x1_ref[...] + modb_ref[:, 5 * D:6 * D] * ff_ref[...], ln2g_ref[...], ln2b_ref[...])

    merged = gate_ref[...].astype(f32) * _dot(ys_ref[...], wsp_ref[...]) + mg_ref[...].astype(f32)
    out_x = _dot(merged.astype(bf16), wo_ref[...])
    x1_ref[...] = _ln(x0a_ref[...] + moda_ref[:, 2 * D:3 * D] * out_x, ln1g_ref[...], ln1b_ref[...])


def _pass2_call(x0a, mod3, sz, xs, bc, yb, mg, gate, dt, alog_col, dsk, ng, wsp, wo, ln1g, ln1b, s_f,
                w1, w3, w2, ln2g, ln2b, t):
    bsz, seq, _ = x0a.shape
    nt = seq // t
    last = bsz * nt - 1
    cur = lambda s: jnp.minimum(s, last)
    prev = lambda s: jnp.maximum(s - 1, 0)
    const = lambda shape: pl.BlockSpec(shape, lambda s: (0,) * len(shape), pipeline_mode=pl.Buffered(1))
    tok = lambda w: pl.BlockSpec((None, t, w), lambda s: (cur(s) // nt, cur(s) % nt, 0))
    return pl.pallas_call(
        functools.partial(_pass2_kernel, nt),
        grid=(bsz * nt + 1,),
        in_specs=[tok(D), pl.BlockSpec((None, 1, D_MOD), lambda s: (cur(s) // nt, 0, 0)),
                  pl.BlockSpec((None, 1, D_MOD), lambda s: (prev(s) // nt, 0, 0)),
                  tok(D), tok(D), tok(2 * GROUPS * STATE), tok(D), tok(D), tok(D),
                  pl.BlockSpec((None, 2 * HEADS, t), lambda s: (cur(s) // nt, 0, cur(s) % nt)),
                  const((2 * HEADS, 1)), const((2, D)), const((1, D)),
                  const((D, D)), const((D, D)), const((1, D)), const((1, D)),
                  pl.BlockSpec((None, GROUPS, STATE, GROUP_W), lambda s: (cur(s) // nt, 0, 0, 0)),
                  const((D, D_FF)), const((D, D_FF)), const((D_FF, D)), const((1, D)), const((1, D))],
        out_specs=pl.BlockSpec((None, t, D), lambda s: (prev(s) // nt, prev(s) % nt, 0)),
        out_shape=jax.ShapeDtypeStruct((bsz, seq, D), f32),
        scratch_shapes=[pltpu.VMEM((GROUPS, STATE, GROUP_W), f32), pltpu.VMEM((t, D), bf16),
                        pltpu.VMEM((t, D), f32), pltpu.VMEM((t, D), bf16),
                        pltpu.VMEM((t, D_FF), bf16), pltpu.VMEM((t, D), f32)],
        compiler_params=pltpu.CompilerParams(dimension_semantics=("arbitrary",), vmem_limit_bytes=VMEM_LIMIT),
        name="pass2_ffn",
    )(x0a, mod3, mod3, sz, xs, bc, yb, mg, gate, dt, alog_col, dsk, ng, wsp, wo, ln1g, ln1b, s_f,
      w1, w3, w2, ln2g, ln2b)


def kernel(x, c, ctx, c_ctx, ln0_g, ln0_b, w_ada, b_ada, w_in, conv_w, conv_b, dt_bias, a_log, d_skip, ssd_norm_g, gm_norm_g, gm_norm_b, w_spatial, b_spatial, b_gate, w_ssd_proj, w_gm_proj, w_out, ln1_g, ln1_b, w_ff1, w_ff3, w_ff2, ln2_g, ln2_b):
    bsz, seq, _ = x.shape
    assert x.shape[2] == D and w_in.shape == (DEPTH, D, D_PROJ) and bsz < 16
    assert seq % 512 == 0 and ctx.shape[1] % CHUNK == 0
    row = lambda a: a.reshape(1, -1)

    w_in0 = w_in[0]
    o_dt = D + XBC
    w_zx = w_in0[:, :o_dt].astype(bf16)
    w_uvg = w_in0[:, o_dt + 2 * HEADS:].astype(bf16)
    w_dtT = w_in0[:, o_dt:o_dt + 2 * HEADS].T.astype(bf16)
    dtb_col = dt_bias[0].reshape(2 * HEADS, 1)
    alog_col = a_log[0].reshape(2 * HEADS, 1)
    dsk = jnp.repeat(d_skip[0], HEAD_DIM, axis=1)
    bsx = jnp.repeat(b_spatial[0].T, GM_GROUP_DIM, axis=1)
    ln0g, ln0b = row(ln0_g), row(ln0_b)

    cc = jnp.concatenate([c, c_ctx[None, :], jnp.zeros((15 - bsz, D), f32)], axis=0)
    mod = _mod_call(cc, w_ada[0], row(b_ada[0]))
    mod3 = mod.reshape(16, 1, D_MOD)
    s_f, s_b = _ctx_call(ctx, mod3[bsz:bsz + 1], ln0g, ln0b, w_zx, w_dtT,
                         conv_w[0], row(conv_b[0]), dtb_col, alog_col)

    t = 256
    x0a, sz, xs, bc, yb, mg, gate, dt = _pass1_call(
        x, mod3, ln0g, ln0b, w_zx, w_uvg, w_dtT, conv_w[0], row(conv_b[0]), dtb_col, alog_col, s_b,
        row(gm_norm_g[0]), row(gm_norm_b[0]), w_spatial[0].astype(bf16), bsx, row(b_gate[0]),
        w_gm_proj[0].astype(bf16), t)
    return _pass2_call(x0a, mod3, sz, xs, bc, yb, mg, gate, dt, alog_col, dsk, row(ssd_norm_g[0]),
                       w_ssd_proj[0].astype(bf16), w_out[0].astype(bf16),
                       row(ln1_g[0]), row(ln1_b[0]), s_f,
                       w_ff1[0].astype(bf16), w_ff3[0].astype(bf16), w_ff2[0].astype(bf16),
                       row(ln2_g[0]), row(ln2_b[0]), 2 * t)
```

```python
import functools

import jax
import jax.numpy as jnp
from jax import lax
from jax.experimental import pallas as pl
from jax.experimental.pallas import tpu as pltpu

f32 = jnp.float32
bf16 = jnp.bfloat16

D = 1024
HEADS = 16
HEAD_DIM = 64
GROUPS = 2
GROUP_W = D // GROUPS
STATE = 128
CONV = 5
CHUNK = 128
XBC = D + 2 * GROUPS * STATE
GM_GROUPS = 8
GM_GROUP_DIM = D // GM_GROUPS
D_FF = 2816
D_MOD = 6 * D
DEPTH = 1
ALPHA = (2 * DEPTH) ** 0.25
LN_EPS = 1e-5
HALO = 8
LANES = 128
NBLK = 256

D_PROJ = D + XBC + 2 * HEADS + 4 * D

VMEM_LIMIT = 58 * 1024 * 1024


def _dot(a, b):
    return jnp.dot(a, b, preferred_element_type=f32)


def _dot_nt(a, b):
    return lax.dot_general(a, b, (((1,), (1,)), ((), ())), preferred_element_type=f32)


def _dot_tn(a, b):
    return lax.dot_general(a, b, (((0,), (0,)), ((), ())), preferred_element_type=f32)


def _norm_rows(xf):
    mu = jnp.mean(xf, axis=-1, keepdims=True)
    xc = xf - mu
    return xc * lax.rsqrt(jnp.mean(xc * xc, axis=-1, keepdims=True) + LN_EPS)


def _ln(xf, g, b):
    return _norm_rows(xf) * g + b


def _sigmoid(x):
    return 0.5 * jnp.tanh(0.5 * x) + 0.5


def _silu(x):
    h = 0.5 * x
    return h * jnp.tanh(h) + h


def _softplus(x):
    return jnp.maximum(x, 0.0) + jnp.log1p(jnp.exp(-jnp.abs(x)))


def _scan_rows(a, tri):
    hi = a.astype(bf16)
    r1 = a - hi.astype(f32)
    mid = r1.astype(bf16)
    lo = (r1 - mid.astype(f32)).astype(bf16)
    cs = _dot(jnp.concatenate([hi, mid, lo], axis=0), tri)
    return cs[0:16] + cs[16:32] + cs[32:48]


def _dir_rows(dt, a_col, fwd):
    k = lax.broadcasted_iota(jnp.int32, (CHUNK, CHUNK), 0)
    j = lax.broadcasted_iota(jnp.int32, (CHUNK, CHUNK), 1)
    tri = jnp.where((k <= j) if fwd else (k >= j), 1.0, 0.0).astype(bf16)
    cum = _scan_rows(dt * a_col, tri)
    total = cum[:, CHUNK - 1:CHUNK] if fwd else cum[:, 0:1]
    return cum, jnp.exp(total - cum) * dt, jnp.exp(cum)


def _to_cols(rows):
    pad = jnp.zeros((LANES - 16 * len(rows), CHUNK), f32)
    return jnp.concatenate(list(rows) + [pad], axis=0).T


def _expand_heads(m):
    lane = lax.broadcasted_iota(jnp.int32, (CHUNK, LANES), 1)
    pieces = []
    for p in range(HEADS // 2):
        pieces.append(jnp.where(lane < HEAD_DIM, m[:, 2 * p:2 * p + 1], m[:, 2 * p + 1:2 * p + 2]))
    return jnp.concatenate(pieces, axis=1)


def _ssd_prep(dt, a_col, fwd):
    cum, wdt, ecum = _dir_rows(dt, a_col, fwd)
    return cum - jnp.log(dt), _to_cols([cum, wdt, ecum])


def _ssd_chunk(prep, xs_bf, xs_f, bm, cm, h_ref, fwd, want_y=True, fillers=()):
    cumj, cols = prep
    fillers = list(fillers)
    ecum_x = _expand_heads(cols[:, 32:48])
    y = None
    if want_y:
        ii = lax.broadcasted_iota(jnp.int32, (CHUNK, CHUNK), 0)
        jj = lax.broadcasted_iota(jnp.int32, (CHUNK, CHUNK), 1)
        keep = (ii >= jj) if fwd else (ii <= jj)
        lane = lax.broadcasted_iota(jnp.int32, (CHUNK, LANES), 1)
        cb = [_dot_nt(cm[:, g * STATE:(g + 1) * STATE], bm[:, g * STATE:(g + 1) * STATE]) for g in range(GROUPS)]
        pieces = []
        for p in range(HEADS // 2):
            g = (2 * p) // (HEADS // GROUPS)
            ws = []
            for h in (2 * p, 2 * p + 1):
                seg = cols[:, h:h + 1] - cumj[h:h + 1, :]
                ws.append(jnp.where(keep, cb[g] * jnp.exp(seg), 0.0))
            lhs = jnp.concatenate(ws, axis=1).astype(bf16)
            x2 = xs_bf[:, p * LANES:(p + 1) * LANES]
            zero = jnp.zeros_like(x2)
            rhs = jnp.concatenate([jnp.where(lane < HEAD_DIM, x2, zero),
                                   jnp.where(lane >= HEAD_DIM, x2, zero)], axis=0)
            pieces.append(_dot(lhs, rhs))
            if fillers:
                fillers.pop(0)()
        y_diag = jnp.concatenate(pieces, axis=1)
        y_off = jnp.concatenate([_dot(cm[:, g * STATE:(g + 1) * STATE], h_ref[g].astype(bf16))
                                 for g in range(GROUPS)], axis=1)
        y = y_diag + y_off * ecum_x
    xw = (xs_f * _expand_heads(cols[:, 16:32])).astype(bf16)
    dec = ecum_x[CHUNK - 1:CHUNK, :] if fwd else ecum_x[0:1, :]
    for g in range(GROUPS):
        st = _dot_tn(bm[:, g * STATE:(g + 1) * STATE], xw[:, g * GROUP_W:(g + 1) * GROUP_W])
        h_ref[g] = dec[:, g * GROUP_W:(g + 1) * GROUP_W] * h_ref[g] + st
    for f in fillers:
        f()
    return y


def _conv_silu(ext_ref, convw_ref, convb_ref, t, c0, c1):
    n = (t + 2 * HALO) // 8
    w = c1 - c0
    p = ext_ref[:, c0:c1].reshape(n, 8, w)
    sub = lax.broadcasted_iota(jnp.int32, (1, 8, w), 1)
    mid = CONV // 2
    acc = convb_ref[:, c0:c1].reshape(1, 1, w) + p[1:n - 1] * convw_ref[mid:mid + 1, c0:c1].reshape(1, 1, w)
    for k in range(CONV):
        s = k - mid
        if s > 0:
            mixed = jnp.where(sub >= s, p[1:n - 1], p[2:n])
            shifted = pltpu.roll(mixed, 8 - s, axis=1)
        elif s < 0:
            mixed = jnp.where(sub < 8 + s, p[1:n - 1], p[0:n - 2])
            shifted = pltpu.roll(mixed, -s, axis=1)
        else:
            continue
        acc = acc + shifted * convw_ref[k:k + 1, c0:c1].reshape(1, 1, w)
    return _silu(acc).reshape(t, w)


def _mod_kernel(c_ref, w_ref, b_ref, o_ref):
    s = _silu(c_ref[...]).astype(bf16)
    o_ref[...] = _dot(s, w_ref[...].astype(bf16)) + b_ref[...]


def _mod_call(cc, w_ada, b_ada):
    nb = 1024
    return pl.pallas_call(
        _mod_kernel,
        grid=(D_MOD // nb,),
        in_specs=[pl.BlockSpec((16, D), lambda n: (0, 0)),
                  pl.BlockSpec((D, nb), lambda n: (0, n)),
                  pl.BlockSpec((1, nb), lambda n: (0, n))],
        out_specs=pl.BlockSpec((16, nb), lambda n: (0, n)),
        out_shape=jax.ShapeDtypeStruct((16, D_MOD), f32),
        compiler_params=pltpu.CompilerParams(dimension_semantics=("arbitrary",), vmem_limit_bytes=VMEM_LIMIT),
        name="adaln_mod",
    )(cc, w_ada, b_ada)


def _ctx_kernel(ctx_ref, mod_ref, ln0g_ref, ln0b_ref, wzx_ref, wdtT_ref, convw_ref, convb_ref, dtb_ref,
                alog_ref, sf_ref, sb_ref, ext_ref, xs_ref, bc_ref):
    cl = ctx_ref.shape[0]
    sh1 = mod_ref[:, 0:D]
    sc1 = mod_ref[:, D:2 * D]
    xm = (_ln(ctx_ref[...], ln0g_ref[...], ln0b_ref[...]) * (1.0 + sc1) + sh1).astype(bf16)
    ext_ref[pl.ds(0, HALO), :] = jnp.zeros((HALO, XBC), f32)
    ext_ref[pl.ds(HALO + cl, HALO), :] = jnp.zeros((HALO, XBC), f32)
    ext_ref[pl.ds(HALO, cl), :] = _dot(xm, wzx_ref[:, D:])
    for c0 in range(0, D, NBLK):
        xs_ref[:, c0:c0 + NBLK] = _conv_silu(ext_ref, convw_ref, convb_ref, cl, c0, c0 + NBLK)
    bc_ref[...] = _conv_silu(ext_ref, convw_ref, convb_ref, cl, D, XBC).astype(bf16)
    dt = _softplus(_dot_nt(wdtT_ref[...], xm) + dtb_ref[...])
    a_col = -jnp.exp(alog_ref[...])
    sf_ref[...] = jnp.zeros(sf_ref.shape, f32)
    sb_ref[...] = jnp.zeros(sb_ref.shape, f32)
    nch = cl // CHUNK
    jobs = [(fwd, h_ref, c, 0 if fwd else HEADS)
            for k in range(nch) for fwd, h_ref, c in ((True, sf_ref, k), (False, sb_ref, nch - 1 - k))]
    preps = [_ssd_prep(dt[r0:r0 + HEADS, c * CHUNK:(c + 1) * CHUNK], a_col[r0:r0 + HEADS], fwd)
             for fwd, _, c, r0 in jobs]
    for prep, (fwd, h_ref, c, _) in zip(preps, jobs):
        rows = slice(c * CHUNK, (c + 1) * CHUNK)
        _ssd_chunk(prep, None, xs_ref[rows, :], bc_ref[rows, 0:GROUPS * STATE], None, h_ref, fwd, want_y=False)


def _ctx_call(ctx, mod3, ln0g, ln0b, w_zx, w_dtT, convw, convb, dtb_col, alog_col):
    bsz, cl, _ = ctx.shape
    full = lambda shape: pl.BlockSpec(shape, lambda b: (0,) * len(shape))
    st_shape = jax.ShapeDtypeStruct((bsz, GROUPS, STATE, GROUP_W), f32)
    st_spec = pl.BlockSpec((None, GROUPS, STATE, GROUP_W), lambda b: (b, 0, 0, 0))
    return pl.pallas_call(
        _ctx_kernel,
        grid=(bsz,),
        in_specs=[pl.BlockSpec((None, cl, D), lambda b: (b, 0, 0)),
                  pl.BlockSpec((None, 1, 2 * D), lambda b: (0, 0, 0)),
                  full((1, D)), full((1, D)), full((D, D + XBC)), full((2 * HEADS, D)),
                  full((CONV, XBC)), full((1, XBC)), full((2 * HEADS, 1)), full((2 * HEADS, 1))],
        out_specs=[st_spec, st_spec],
        out_shape=[st_shape, st_shape],
        scratch_shapes=[pltpu.VMEM((cl + 2 * HALO, XBC), f32), pltpu.VMEM((cl, D), f32),
                        pltpu.VMEM((cl, 2 * GROUPS * STATE), bf16)],
        compiler_params=pltpu.CompilerParams(dimension_semantics=("arbitrary",), vmem_limit_bytes=VMEM_LIMIT),
        name="ctx_states",
    )(ctx, mod3, ln0g, ln0b, w_zx, w_dtT, convw, convb, dtb_col, alog_col)


def _pass1_kernel(x_ref, xp_ref, xn_ref, mod_ref, ln0g_ref, ln0b_ref, wzx_ref, wuvg_ref, wdtT_ref, convw_ref,
                  convb_ref, dtb_ref, alog_ref, sb_ref, gmg_ref, gmb_ref, ws_ref, bsx_ref, bg_ref, wgp_ref,
                  x0a_ref, sz_ref, xs_ref, bc_ref, yb_ref, mg_ref, gate_ref, dt_ref,
                  ext_ref, hb_ref, xsf_ref, ug_ref, vg_ref, vn_ref, ygm_ref, gg_ref):
    t = x_ref.shape[0]
    nch = t // CHUNK
    step = pl.program_id(1)
    nt = pl.num_programs(1)
    tile = nt - 1 - step

    @pl.when(step == 0)
    def _():
        hb_ref[...] = sb_ref[...]

    g0 = ln0g_ref[...]
    b0 = ln0b_ref[...]
    sc1p = 1.0 + mod_ref[:, D:2 * D]
    gm = g0 * sc1p
    bm = b0 * sc1p + mod_ref[:, 0:D]

    xn = _norm_rows(x_ref[...])
    x0a_ref[...] = xn * (ALPHA * g0) + ALPHA * b0
    xm_f = xn * gm + bm
    xm = xm_f.astype(bf16)
    ext = jnp.concatenate([_norm_rows(xp_ref[...]) * gm + bm, xm_f, _norm_rows(xn_ref[...]) * gm + bm],
                          axis=0).astype(bf16)
    row = lax.broadcasted_iota(jnp.int32, (t + 2 * HALO, 1), 0)
    valid = jnp.logical_and(jnp.logical_or(row >= HALO, tile > 0), jnp.logical_or(row < HALO + t, tile < nt - 1))

    def ext_task(c0):
        def run():
            w = XBC // 3
            ext_ref[:, c0:c0 + w] = jnp.where(valid, _dot(ext, wzx_ref[:, D + c0:D + c0 + w]), 0.0)
        return run

    def proj_task(w_ref, w0, c0, dst_ref, fn):
        def run():
            cols = slice(c0, c0 + NBLK)
            dst_ref[:, cols] = fn(_dot(xm, w_ref[:, w0 + c0:w0 + c0 + NBLK]), cols).astype(dst_ref.dtype)
        return run

    def conv_task(c0):
        def run():
            blk = _conv_silu(ext_ref, convw_ref, convb_ref, t, c0, c0 + LANES)
            if c0 < D:
                xsf_ref[:, c0:c0 + LANES] = blk
                xs_ref[:, c0:c0 + LANES] = blk.astype(bf16)
            else:
                bc_ref[:, c0 - D:c0 - D + LANES] = blk.astype(bf16)
        return run

    def vnorm_task(c):
        def run():
            rows = slice(c * CHUNK, (c + 1) * CHUNK)
            vn_ref[rows, :] = _ln(vg_ref[rows, :], gmg_ref[...], gmb_ref[...]).astype(bf16)
        return run

    def gating_task(g):
        def run():
            cols = slice(g * GM_GROUP_DIM, (g + 1) * GM_GROUP_DIM)
            rhs = jnp.concatenate([vn_ref[c * CHUNK:(c + 1) * CHUNK, cols] for c in range(nch)], axis=1)
            mixed = _dot(ws_ref[g], rhs)
            for c in range(nch):
                rows = slice(c * CHUNK, (c + 1) * CHUNK)
                ygm_ref[rows, cols] = (ug_ref[rows, cols].astype(f32)
                                       * (mixed[:, c * GM_GROUP_DIM:(c + 1) * GM_GROUP_DIM]
                                          + bsx_ref[:, cols])).astype(bf16)
        return run

    blocks = range(0, D, NBLK)
    gelu = lambda r, cols: jax.nn.gelu(r)
    u_t = [proj_task(wuvg_ref, 0, c0, ug_ref, gelu) for c0 in blocks]
    v_t = [proj_task(wuvg_ref, D, c0, vg_ref, gelu) for c0 in blocks]
    z_t = [proj_task(wzx_ref, 0, c0, sz_ref, lambda r, cols: _silu(r)) for c0 in blocks]
    gs_t = [proj_task(wuvg_ref, 2 * D, c0, gate_ref, lambda r, cols: _sigmoid(r + bg_ref[:, cols]))
            for c0 in blocks]
    gg_t = [proj_task(wuvg_ref, 3 * D, c0, gg_ref,
                      lambda r, cols: _sigmoid(r + bg_ref[:, D + cols.start:D + cols.stop])) for c0 in blocks]

    def gm_task(c0):
        def run():
            cols = slice(c0, c0 + NBLK)
            mg_ref[:, cols] = (gg_ref[:, cols].astype(f32) * _dot(ygm_ref[...], wgp_ref[:, cols])).astype(bf16)
        return run

    gm_t = [gm_task(c0) for c0 in blocks]
    e_t = [ext_task(c0) for c0 in range(0, XBC, XBC // 3)]
    c_t = [conv_task(c0) for c0 in range(0, XBC, LANES)]
    n_t = [vnorm_task(c) for c in range(nch)]
    s_t = [gating_task(g) for g in range(GM_GROUPS)]
    g_t = gg_t + [t_ for pair in zip(gs_t, gm_t) for t_ in pair]
    fill = (len(g_t) - len(z_t)) // nch
    assert (len(v_t), len(c_t), len(s_t)) == (4, 12, 8) and fill * nch == len(g_t) - len(z_t)

    for task in (e_t[0], v_t[0], c_t[0], e_t[1], c_t[1], v_t[1], c_t[2], e_t[2], c_t[3], v_t[2], c_t[4], v_t[3],
                 c_t[5]):
        task()
    dt = _softplus(_dot_nt(wdtT_ref[...], xm) + dtb_ref[...])
    dt_ref[...] = dt
    a_col = -jnp.exp(alog_ref[...])
    preps = [_ssd_prep(dt[HEADS:, c * CHUNK:(c + 1) * CHUNK], a_col[HEADS:], False) for c in range(nch)]
    for task in (u_t[0], c_t[6], u_t[1], c_t[7], *n_t[:nch // 2], u_t[2], c_t[8], *n_t[nch // 2:], u_t[3], c_t[9],
                 z_t[0], c_t[10], g_t[0], c_t[11], s_t[0], s_t[1]):
        task()
    for i in range(1, 4):
        for task in (z_t[i], g_t[i], s_t[2 * i], s_t[2 * i + 1]):
            task()
    rest = g_t[len(z_t):]
    for i, c in enumerate(range(nch - 1, -1, -1)):
        rows = slice(c * CHUNK, (c + 1) * CHUNK)
        y = _ssd_chunk(preps[c], xs_ref[rows, :], xsf_ref[rows, :], bc_ref[rows, 0:GROUPS * STATE],
                       bc_ref[rows, GROUPS * STATE:], hb_ref, fwd=False, fillers=rest[i * fill:(i + 1) * fill])
        yb_ref[rows, :] = y.astype(bf16)


def _pass1_call(x, mod3, ln0g, ln0b, w_zx, w_uvg, w_dtT, convw, convb, dtb_col, alog_col, s_b, gmg, gmb, ws, bsx,
                bg, wgp, t):
    bsz, seq, _ = x.shape
    nt = seq // t
    hb = t // HALO
    nhb = seq // HALO
    const = lambda shape: pl.BlockSpec(shape, lambda b, s: (0,) * len(shape), pipeline_mode=pl.Buffered(1))
    tok = lambda w: pl.BlockSpec((None, t, w), lambda b, s: (b, nt - 1 - s, 0))
    act = lambda w, dt_: jax.ShapeDtypeStruct((bsz, seq, w), dt_)
    return pl.pallas_call(
        _pass1_kernel,
        grid=(bsz, nt),
        in_specs=[tok(D),
                  pl.BlockSpec((None, HALO, D), lambda b, s: (b, jnp.maximum((nt - 1 - s) * hb - 1, 0), 0)),
                  pl.BlockSpec((None, HALO, D), lambda b, s: (b, jnp.minimum((nt - s) * hb, nhb - 1), 0)),
                  pl.BlockSpec((None, 1, 2 * D), lambda b, s: (b, 0, 0)),
                  const((1, D)), const((1, D)), const((D, D + XBC)), const((D, 4 * D)), const((2 * HEADS, D)),
                  const((CONV, XBC)), const((1, XBC)), const((2 * HEADS, 1)), const((2 * HEADS, 1)),
                  pl.BlockSpec((None, GROUPS, STATE, GROUP_W), lambda b, s: (b, 0, 0, 0)),
                  const((1, D)), const((1, D)), const((GM_GROUPS, CHUNK, CHUNK)), const((CHUNK, D)),
                  const((1, 2 * D)), const((D, D))],
        out_specs=[tok(D), tok(D), tok(D), tok(2 * GROUPS * STATE), tok(D), tok(D), tok(D),
                   pl.BlockSpec((None, 2 * HEADS, t), lambda b, s: (b, 0, nt - 1 - s))],
        out_shape=[act(D, f32), act(D, bf16), act(D, bf16), act(2 * GROUPS * STATE, bf16), act(D, bf16),
                   act(D, bf16), act(D, bf16), jax.ShapeDtypeStruct((bsz, 2 * HEADS, seq), f32)],
        scratch_shapes=[pltpu.VMEM((t + 2 * HALO, XBC), f32), pltpu.VMEM((GROUPS, STATE, GROUP_W), f32),
                        pltpu.VMEM((t, D), f32), pltpu.VMEM((t, D), bf16), pltpu.VMEM((t, D), f32),
                        pltpu.VMEM((t, D), bf16), pltpu.VMEM((t, D), bf16), pltpu.VMEM((t, D), bf16)],
        compiler_params=pltpu.CompilerParams(dimension_semantics=("arbitrary", "arbitrary"),
                                             vmem_limit_bytes=VMEM_LIMIT),
        name="pass1_bwd",
    )(x, x, x, mod3, ln0g, ln0b, w_zx, w_uvg, w_dtT, convw, convb, dtb_col, alog_col, s_b, gmg, gmb, ws, bsx, bg,
      wgp)


def _pass2_kernel(nt, x0a_ref, moda_ref, modb_ref, sz_ref, xs_ref, bc_ref, yb_ref, mg_ref, gate_ref, dt_ref,
                  alog_ref, dsk_ref, ng_ref, wsp_ref, wo_ref, ln1g_ref, ln1b_ref, sf_ref,
                  w1_ref, w3_ref, w2_ref, ln2g_ref, ln2b_ref,
                  o_ref, hf_ref, ys_ref, x1_ref, hm_ref, a_ref, ff_ref):
    t = x0a_ref.shape[0]
    nch = t // CHUNK
    step = pl.program_id(0)

    @pl.when(step == 0)
    def _():
        x1_ref[...] = jnp.zeros(x1_ref.shape, f32)

    @pl.when(step % nt == 0)
    def _():
        hf_ref[...] = sf_ref[...]

    hm_ref[...] = (x1_ref[...] * (1.0 + modb_ref[:, 4 * D:5 * D]) + modb_ref[:, 3 * D:4 * D]).astype(bf16)

    def up_task(n0):
        def run():
            cols = slice(n0, n0 + NBLK)
            a_ref[:, cols] = (_silu(_dot(hm_ref[...], w1_ref[:, cols])) * _dot(hm_ref[...], w3_ref[:, cols])).astype(bf16)
        return run

    def down_task(c0):
        def run():
            ff_ref[:, c0:c0 + NBLK] = _dot(a_ref[...], w2_ref[:, c0:c0 + NBLK])
        return run

    tasks = [up_task(n0) for n0 in range(0, D_FF, NBLK)] + [down_task(c0) for c0 in range(0, D, NBLK)]
    per = -(-len(tasks) // nch)

    a_col = -jnp.exp(alog_ref[...])
    dt = dt_ref[...]
    preps = [_ssd_prep(dt[0:HEADS, c * CHUNK:(c + 1) * CHUNK], a_col[0:HEADS], True) for c in range(nch)]
    dsk = dsk_ref[0:1, :] + dsk_ref[1:2, :]
    for c in range(nch):
        rows = slice(c * CHUNK, (c + 1) * CHUNK)
        xs_bf = xs_ref[rows, :]
        xs_f = xs_bf.astype(f32)
        y = _ssd_chunk(preps[c], xs_bf, xs_f, bc_ref[rows, 0:GROUPS * STATE], bc_ref[rows, GROUPS * STATE:],
                       hf_ref, fwd=True, fillers=tasks[c * per:(c + 1) * per])
        y = y + yb_ref[rows, :].astype(f32) + xs_f * dsk
        hh = y * sz_ref[rows, :].astype(f32)
        hh = hh * lax.rsqrt(jnp.mean(hh * hh, axis=-1, keepdims=True) + LN_EPS) * ng_ref[...]
        ys_ref[rows, :] = hh.astype(bf16)

    o_ref[...] = _ln(ALPHA * x1_ref[...] + modb_ref[:, 5 * D:6 * D] * ff_ref[...], ln2g_ref[...], ln2b_ref[...])

    merged = gate_ref[...].astype(f32) * _dot(ys_ref[...], wsp_ref[...]) + mg_ref[...].astype(f32)
    out_x = _dot(merged.astype(bf16), wo_ref[...])
    x1_ref[...] = _ln(x0a_ref[...] + moda_ref[:, 2 * D:3 * D] * out_x, ln1g_ref[...], ln1b_ref[...])


def _pass2_call(x0a, mod3, sz, xs, bc, yb, mg, gate, dt, alog_col, dsk, ng, wsp, wo, ln1g, ln1b, s_f,
                w1, w3, w2, ln2g, ln2b, t):
    bsz, seq, _ = x0a.shape
    nt = seq // t
    last = bsz * nt - 1
    cur = lambda s: jnp.minimum(s, last)
    prev = lambda s: jnp.maximum(s - 1, 0)
    const = lambda shape: pl.BlockSpec(shape, lambda s: (0,) * len(shape), pipeline_mode=pl.Buffered(1))
    tok = lambda w: pl.BlockSpec((None, t, w), lambda s: (cur(s) // nt, cur(s) % nt, 0))
    return pl.pallas_call(
        functools.partial(_pass2_kernel, nt),
        grid=(bsz * nt + 1,),
        in_specs=[tok(D), pl.BlockSpec((None, 1, D_MOD), lambda s: (cur(s) // nt, 0, 0)),
                  pl.BlockSpec((None, 1, D_MOD), lambda s: (prev(s) // nt, 0, 0)),
                  tok(D), tok(D), tok(2 * GROUPS * STATE), tok(D), tok(D), tok(D),
                  pl.BlockSpec((None, 2 * HEADS, t), lambda s: (cur(s) // nt, 0, cur(s) % nt)),
                  const((2 * HEADS, 1)), const((2, D)), const((1, D)),
                  const((D, D)), const((D, D)), const((1, D)), const((1, D)),
                  pl.BlockSpec((None, GROUPS, STATE, GROUP_W), lambda s: (cur(s) // nt, 0, 0, 0)),
                  const((D, D_FF)), const((D, D_FF)), const((D_FF, D)), const((1, D)), const((1, D))],
        out_specs=pl.BlockSpec((None, t, D), lambda s: (prev(s) // nt, prev(s) % nt, 0)),
        out_shape=jax.ShapeDtypeStruct((bsz, seq, D), f32),
        scratch_shapes=[pltpu.VMEM((GROUPS, STATE, GROUP_W), f32), pltpu.VMEM((t, D), bf16),
                        pltpu.VMEM((t, D), f32), pltpu.VMEM((t, D), bf16),
                        pltpu.VMEM((t, D_FF), bf16), pltpu.VMEM((t, D), f32)],
        compiler_params=pltpu.CompilerParams(dimension_semantics=("arbitrary",), vmem_limit_bytes=VMEM_LIMIT),
        name="pass2_ffn",
    )(x0a, mod3, mod3, sz, xs, bc, yb, mg, gate, dt, alog_col, dsk, ng, wsp, wo, ln1g, ln1b, s_f,
      w1, w3, w2, ln2g, ln2b)


def kernel(x, c, ctx, c_ctx, ln0_g, ln0_b, w_ada, b_ada, w_in, conv_w, conv_b, dt_bias, a_log, d_skip, ssd_norm_g, gm_norm_g, gm_norm_b, w_spatial, b_spatial, b_gate, w_ssd_proj, w_gm_proj, w_out, ln1_g, ln1_b, w_ff1, w_ff3, w_ff2, ln2_g, ln2_b):
    bsz, seq, _ = x.shape
    assert x.shape[2] == D and w_in.shape == (DEPTH, D, D_PROJ) and bsz < 16
    assert seq % 512 == 0 and ctx.shape[1] % CHUNK == 0
    row = lambda a: a.reshape(1, -1)

    w_in0 = w_in[0]
    o_dt = D + XBC
    w_zx = w_in0[:, :o_dt].astype(bf16)
    w_uvg = w_in0[:, o_dt + 2 * HEADS:].astype(bf16)
    w_dtT = w_in0[:, o_dt:o_dt + 2 * HEADS].T.astype(bf16)
    dtb_col = dt_bias[0].reshape(2 * HEADS, 1)
    alog_col = a_log[0].reshape(2 * HEADS, 1)
    dsk = jnp.repeat(d_skip[0], HEAD_DIM, axis=1)
    bsx = jnp.repeat(b_spatial[0].T, GM_GROUP_DIM, axis=1)
    ln0g, ln0b = row(ln0_g), row(ln0_b)

    cc = jnp.concatenate([c, c_ctx[None, :], jnp.zeros((15 - bsz, D), f32)], axis=0)
    mod = _mod_call(cc, w_ada[0], row(b_ada[0]))
    mod3 = mod.reshape(16, 1, D_MOD)
    s_f, s_b = _ctx_call(ctx, mod3[bsz:bsz + 1], ln0g, ln0b, w_zx, w_dtT,
                         conv_w[0], row(conv_b[0]), dtb_col, alog_col)

    t = 256
    x0a, sz, xs, bc, yb, mg, gate, dt = _pass1_call(
        x, mod3, ln0g, ln0b, w_zx, w_uvg, w_dtT, conv_w[0], row(conv_b[0]), dtb_col, alog_col, s_b,
        row(gm_norm_g[0]), row(gm_norm_b[0]), w_spatial[0].astype(bf16), bsx, row(b_gate[0]),
        w_gm_proj[0].astype(bf16), 2 * t)
    return _pass2_call(x0a, mod3, sz, xs, bc, yb, mg, gate, dt, alog_col, dsk, row(ssd_norm_g[0]),
                       w_ssd_proj[0].astype(bf16), w_out[0].astype(bf16),
                       row(ln1_g[0]), row(ln1_b[0]), s_f,
                       w_ff1[0].astype(bf16), w_ff3[0].astype(bf16), w_ff2[0].astype(bf16),
                       row(ln2_g[0]), row(ln2_b[0]), 2 * t)
```

```python
import functools

import jax
import jax.numpy as jnp
from jax import lax
from jax.experimental import pallas as pl
from jax.experimental.pallas import tpu as pltpu

f32 = jnp.float32
bf16 = jnp.bfloat16

D = 1024
HEADS = 16
HEAD_DIM = 64
GROUPS = 2
GROUP_W = D // GROUPS
STATE = 128
CONV = 5
CHUNK = 128
XBC = D + 2 * GROUPS * STATE
GM_GROUPS = 8
GM_GROUP_DIM = D // GM_GROUPS
D_FF = 2816
D_MOD = 6 * D
DEPTH = 1
ALPHA = (2 * DEPTH) ** 0.25
LN_EPS = 1e-5
HALO = 8
LANES = 128
NBLK = 256

D_PROJ = D + XBC + 2 * HEADS + 4 * D

VMEM_LIMIT = 56 * 1024 * 1024


def _dot(a, b):
    return jnp.dot(a, b, preferred_element_type=f32)


def _dot_nt(a, b):
    return lax.dot_general(a, b, (((1,), (1,)), ((), ())), preferred_element_type=f32)


def _dot_tn(a, b):
    return lax.dot_general(a, b, (((0,), (0,)), ((), ())), preferred_element_type=f32)


def _norm_rows(xf):
    mu = jnp.mean(xf, axis=-1, keepdims=True)
    xc = xf - mu
    return xc * lax.rsqrt(jnp.mean(xc * xc, axis=-1, keepdims=True) + LN_EPS)


def _ln(xf, g, b):
    return _norm_rows(xf) * g + b


def _sigmoid(x):
    return 0.5 * jnp.tanh(0.5 * x) + 0.5


def _silu(x):
    h = 0.5 * x
    return h * jnp.tanh(h) + h


def _softplus(x):
    return jnp.maximum(x, 0.0) + jnp.log1p(jnp.exp(-jnp.abs(x)))


def _scan_rows(a, tri):
    hi = a.astype(bf16)
    r1 = a - hi.astype(f32)
    mid = r1.astype(bf16)
    lo = (r1 - mid.astype(f32)).astype(bf16)
    cs = _dot(jnp.concatenate([hi, mid, lo], axis=0), tri)
    return cs[0:16] + cs[16:32] + cs[32:48]


def _dir_rows(dt, a_col, fwd):
    k = lax.broadcasted_iota(jnp.int32, (CHUNK, CHUNK), 0)
    j = lax.broadcasted_iota(jnp.int32, (CHUNK, CHUNK), 1)
    tri = jnp.where((k <= j) if fwd else (k >= j), 1.0, 0.0).astype(bf16)
    cum = _scan_rows(dt * a_col, tri)
    total = cum[:, CHUNK - 1:CHUNK] if fwd else cum[:, 0:1]
    return cum, jnp.exp(total - cum) * dt, jnp.exp(cum)


def _to_cols(rows):
    pad = jnp.zeros((LANES - 16 * len(rows), CHUNK), f32)
    return jnp.concatenate(list(rows) + [pad], axis=0).T


def _expand_heads(m):
    lane = lax.broadcasted_iota(jnp.int32, (CHUNK, LANES), 1)
    pieces = []
    for p in range(HEADS // 2):
        pieces.append(jnp.where(lane < HEAD_DIM, m[:, 2 * p:2 * p + 1], m[:, 2 * p + 1:2 * p + 2]))
    return jnp.concatenate(pieces, axis=1)


def _ssd_prep(dt, a_col, fwd):
    cum, wdt, ecum = _dir_rows(dt, a_col, fwd)
    return cum - jnp.log(dt), _to_cols([cum, wdt, ecum])


def _ssd_chunk(prep, xs_bf, xs_f, bm, cm, h_ref, fwd, want_y=True, fillers=()):
    cumj, cols = prep
    fillers = list(fillers)
    ecum_x = _expand_heads(cols[:, 32:48])
    y = None
    if want_y:
        ii = lax.broadcasted_iota(jnp.int32, (CHUNK, CHUNK), 0)
        jj = lax.broadcasted_iota(jnp.int32, (CHUNK, CHUNK), 1)
        keep = (ii >= jj) if fwd else (ii <= jj)
        lane = lax.broadcasted_iota(jnp.int32, (CHUNK, LANES), 1)
        cb = [_dot_nt(cm[:, g * STATE:(g + 1) * STATE], bm[:, g * STATE:(g + 1) * STATE]) for g in range(GROUPS)]
        pieces = []
        for p in range(HEADS // 2):
            g = (2 * p) // (HEADS // GROUPS)
            ws = []
            for h in (2 * p, 2 * p + 1):
                seg = cols[:, h:h + 1] - cumj[h:h + 1, :]
                ws.append(jnp.where(keep, cb[g] * jnp.exp(seg), 0.0))
            lhs = jnp.concatenate(ws, axis=1).astype(bf16)
            x2 = xs_bf[:, p * LANES:(p + 1) * LANES]
            zero = jnp.zeros_like(x2)
            rhs = jnp.concatenate([jnp.where(lane < HEAD_DIM, x2, zero),
                                   jnp.where(lane >= HEAD_DIM, x2, zero)], axis=0)
            pieces.append(_dot(lhs, rhs))
            if fillers:
                fillers.pop(0)()
        y_diag = jnp.concatenate(pieces, axis=1)
        y_off = jnp.concatenate([_dot(cm[:, g * STATE:(g + 1) * STATE], h_ref[g].astype(bf16))
                                 for g in range(GROUPS)], axis=1)
        y = y_diag + y_off * ecum_x
    xw = (xs_f * _expand_heads(cols[:, 16:32])).astype(bf16)
    dec = ecum_x[CHUNK - 1:CHUNK, :] if fwd else ecum_x[0:1, :]
    for g in range(GROUPS):
        st = _dot_tn(bm[:, g * STATE:(g + 1) * STATE], xw[:, g * GROUP_W:(g + 1) * GROUP_W])
        h_ref[g] = dec[:, g * GROUP_W:(g + 1) * GROUP_W] * h_ref[g] + st
    for f in fillers:
        f()
    return y


def _conv_silu(ext_ref, convw_ref, convb_ref, t, c0, c1):
    n = (t + 2 * HALO) // 8
    w = c1 - c0
    p = ext_ref[:, c0:c1].reshape(n, 8, w)
    sub = lax.broadcasted_iota(jnp.int32, (1, 8, w), 1)
    mid = CONV // 2
    acc = convb_ref[:, c0:c1].reshape(1, 1, w) + p[1:n - 1] * convw_ref[mid:mid + 1, c0:c1].reshape(1, 1, w)
    for k in range(CONV):
        s = k - mid
        if s > 0:
            mixed = jnp.where(sub >= s, p[1:n - 1], p[2:n])
            shifted = pltpu.roll(mixed, 8 - s, axis=1)
        elif s < 0:
            mixed = jnp.where(sub < 8 + s, p[1:n - 1], p[0:n - 2])
            shifted = pltpu.roll(mixed, -s, axis=1)
        else:
            continue
        acc = acc + shifted * convw_ref[k:k + 1, c0:c1].reshape(1, 1, w)
    return _silu(acc).reshape(t, w)


def _mod_kernel(c_ref, w_ref, b_ref, o_ref):
    s = _silu(c_ref[...]).astype(bf16)
    o_ref[...] = _dot(s, w_ref[...].astype(bf16)) + b_ref[...]


def _mod_call(cc, w_ada, b_ada):
    nb = 1024
    return pl.pallas_call(
        _mod_kernel,
        grid=(D_MOD // nb,),
        in_specs=[pl.BlockSpec((16, D), lambda n: (0, 0)),
                  pl.BlockSpec((D, nb), lambda n: (0, n)),
                  pl.BlockSpec((1, nb), lambda n: (0, n))],
        out_specs=pl.BlockSpec((16, nb), lambda n: (0, n)),
        out_shape=jax.ShapeDtypeStruct((16, D_MOD), f32),
        compiler_params=pltpu.CompilerParams(dimension_semantics=("arbitrary",), vmem_limit_bytes=VMEM_LIMIT),
        name="adaln_mod",
    )(cc, w_ada, b_ada)


def _ctx_kernel(ctx_ref, mod_ref, ln0g_ref, ln0b_ref, wzx_ref, wdtT_ref, convw_ref, convb_ref, dtb_ref,
                alog_ref, sf_ref, sb_ref, ext_ref, xs_ref, bc_ref):
    cl = ctx_ref.shape[0]
    sh1 = mod_ref[:, 0:D]
    sc1 = mod_ref[:, D:2 * D]
    xm = (_ln(ctx_ref[...], ln0g_ref[...], ln0b_ref[...]) * (1.0 + sc1) + sh1).astype(bf16)
    ext_ref[pl.ds(0, HALO), :] = jnp.zeros((HALO, XBC), f32)
    ext_ref[pl.ds(HALO + cl, HALO), :] = jnp.zeros((HALO, XBC), f32)
    ext_ref[pl.ds(HALO, cl), :] = _dot(xm, wzx_ref[:, D:])
    for c0 in range(0, D, NBLK):
        xs_ref[:, c0:c0 + NBLK] = _conv_silu(ext_ref, convw_ref, convb_ref, cl, c0, c0 + NBLK)
    bc_ref[...] = _conv_silu(ext_ref, convw_ref, convb_ref, cl, D, XBC).astype(bf16)
    dt = _softplus(_dot_nt(wdtT_ref[...], xm) + dtb_ref[...])
    a_col = -jnp.exp(alog_ref[...])
    sf_ref[...] = jnp.zeros(sf_ref.shape, f32)
    sb_ref[...] = jnp.zeros(sb_ref.shape, f32)
    nch = cl // CHUNK
    jobs = [(fwd, h_ref, c, 0 if fwd else HEADS)
            for k in range(nch) for fwd, h_ref, c in ((True, sf_ref, k), (False, sb_ref, nch - 1 - k))]
    preps = [_ssd_prep(dt[r0:r0 + HEADS, c * CHUNK:(c + 1) * CHUNK], a_col[r0:r0 + HEADS], fwd)
             for fwd, _, c, r0 in jobs]
    for prep, (fwd, h_ref, c, _) in zip(preps, jobs):
        rows = slice(c * CHUNK, (c + 1) * CHUNK)
        _ssd_chunk(prep, None, xs_ref[rows, :], bc_ref[rows, 0:GROUPS * STATE], None, h_ref, fwd, want_y=False)


def _ctx_call(ctx, mod3, ln0g, ln0b, w_zx, w_dtT, convw, convb, dtb_col, alog_col):
    bsz, cl, _ = ctx.shape
    full = lambda shape: pl.BlockSpec(shape, lambda b: (0,) * len(shape))
    st_shape = jax.ShapeDtypeStruct((bsz, GROUPS, STATE, GROUP_W), f32)
    st_spec = pl.BlockSpec((None, GROUPS, STATE, GROUP_W), lambda b: (b, 0, 0, 0))
    return pl.pallas_call(
        _ctx_kernel,
        grid=(bsz,),
        in_specs=[pl.BlockSpec((None, cl, D), lambda b: (b, 0, 0)),
                  pl.BlockSpec((None, 1, 2 * D), lambda b: (0, 0, 0)),
                  full((1, D)), full((1, D)), full((D, D + XBC)), full((2 * HEADS, D)),
                  full((CONV, XBC)), full((1, XBC)), full((2 * HEADS, 1)), full((2 * HEADS, 1))],
        out_specs=[st_spec, st_spec],
        out_shape=[st_shape, st_shape],
        scratch_shapes=[pltpu.VMEM((cl + 2 * HALO, XBC), f32), pltpu.VMEM((cl, D), f32),
                        pltpu.VMEM((cl, 2 * GROUPS * STATE), bf16)],
        compiler_params=pltpu.CompilerParams(dimension_semantics=("arbitrary",), vmem_limit_bytes=VMEM_LIMIT),
        name="ctx_states",
    )(ctx, mod3, ln0g, ln0b, w_zx, w_dtT, convw, convb, dtb_col, alog_col)


PACKED = (D, D, 2 * GROUPS * STATE, D, D, D)


def _unpack(ref):
    views, start = [], 0
    for width in PACKED:
        views.append(ref.at[:, start:start + width])
        start += width
    return views


def _pass1_kernel(x_ref, xp_ref, xn_ref, mod_ref, ln0g_ref, ln0b_ref, wzx_ref, wuvg_ref, wdtT_ref, convw_ref,
                  convb_ref, dtb_ref, alog_ref, sb_ref, gmg_ref, gmb_ref, ws_ref, bsx_ref, bg_ref, wgp_ref,
                  x0a_ref, pk_ref, dt_ref,
                  ext_ref, hb_ref, xsf_ref, ug_ref, vg_ref, vn_ref, ygm_ref, gg_ref):
    sz_ref, xs_ref, bc_ref, yb_ref, mg_ref, gate_ref = _unpack(pk_ref)
    t = x_ref.shape[0]
    nch = t // CHUNK
    step = pl.program_id(1)
    nt = pl.num_programs(1)
    tile = nt - 1 - step

    @pl.when(step == 0)
    def _():
        hb_ref[...] = sb_ref[...]

    g0 = ln0g_ref[...]
    b0 = ln0b_ref[...]
    sc1p = 1.0 + mod_ref[:, D:2 * D]
    gm = g0 * sc1p
    bm = b0 * sc1p + mod_ref[:, 0:D]

    xn = _norm_rows(x_ref[...])
    x0a_ref[...] = xn * (ALPHA * g0) + ALPHA * b0
    xm_f = xn * gm + bm
    xm = xm_f.astype(bf16)
    ext = jnp.concatenate([_norm_rows(xp_ref[...]) * gm + bm, xm_f, _norm_rows(xn_ref[...]) * gm + bm],
                          axis=0).astype(bf16)
    row = lax.broadcasted_iota(jnp.int32, (t + 2 * HALO, 1), 0)
    valid = jnp.logical_and(jnp.logical_or(row >= HALO, tile > 0), jnp.logical_or(row < HALO + t, tile < nt - 1))

    def ext_task(c0):
        def run():
            w = XBC // 3
            ext_ref[:, c0:c0 + w] = jnp.where(valid, _dot(ext, wzx_ref[:, D + c0:D + c0 + w]), 0.0)
        return run

    def proj_task(w_ref, w0, c0, dst_ref, fn):
        def run():
            cols = slice(c0, c0 + NBLK)
            dst_ref[:, cols] = fn(_dot(xm, w_ref[:, w0 + c0:w0 + c0 + NBLK]), cols).astype(dst_ref.dtype)
        return run

    def conv_task(c0):
        def run():
            blk = _conv_silu(ext_ref, convw_ref, convb_ref, t, c0, c0 + LANES)
            if c0 < D:
                xsf_ref[:, c0:c0 + LANES] = blk
                xs_ref[:, c0:c0 + LANES] = blk.astype(bf16)
            else:
                bc_ref[:, c0 - D:c0 - D + LANES] = blk.astype(bf16)
        return run

    def vnorm_task(c):
        def run():
            rows = slice(c * CHUNK, (c + 1) * CHUNK)
            vn_ref[rows, :] = _ln(vg_ref[rows, :], gmg_ref[...], gmb_ref[...]).astype(bf16)
        return run

    def gating_task(g):
        def run():
            cols = slice(g * GM_GROUP_DIM, (g + 1) * GM_GROUP_DIM)
            rhs = jnp.concatenate([vn_ref[c * CHUNK:(c + 1) * CHUNK, cols] for c in range(nch)], axis=1)
            mixed = _dot(ws_ref[g], rhs)
            for c in range(nch):
                rows = slice(c * CHUNK, (c + 1) * CHUNK)
                ygm_ref[rows, cols] = (ug_ref[rows, cols].astype(f32)
                                       * (mixed[:, c * GM_GROUP_DIM:(c + 1) * GM_GROUP_DIM]
                                          + bsx_ref[:, cols])).astype(bf16)
        return run

    blocks = range(0, D, NBLK)
    gelu = lambda r, cols: jax.nn.gelu(r)
    u_t = [proj_task(wuvg_ref, 0, c0, ug_ref, gelu) for c0 in blocks]
    v_t = [proj_task(wuvg_ref, D, c0, vg_ref, gelu) for c0 in blocks]
    z_t = [proj_task(wzx_ref, 0, c0, sz_ref, lambda r, cols: _silu(r)) for c0 in blocks]
    gs_t = [proj_task(wuvg_ref, 2 * D, c0, gate_ref, lambda r, cols: _sigmoid(r + bg_ref[:, cols]))
            for c0 in blocks]
    gg_t = [proj_task(wuvg_ref, 3 * D, c0, gg_ref,
                      lambda r, cols: _sigmoid(r + bg_ref[:, D + cols.start:D + cols.stop])) for c0 in blocks]

    def gm_task(c0):
        def run():
            cols = slice(c0, c0 + NBLK)
            mg_ref[:, cols] = (gg_ref[:, cols] * _dot(ygm_ref[...], wgp_ref[:, cols])).astype(bf16)
        return run

    gm_t = [gm_task(c0) for c0 in blocks]
    e_t = [ext_task(c0) for c0 in range(0, XBC, XBC // 3)]
    c_t = [conv_task(c0) for c0 in range(0, XBC, LANES)]
    n_t = [vnorm_task(c) for c in range(nch)]
    s_t = [gating_task(g) for g in range(GM_GROUPS)]
    g_t = gg_t + [t_ for pair in zip(gs_t, gm_t) for t_ in pair]
    fill = (len(g_t) - len(z_t)) // nch
    assert (len(v_t), len(c_t), len(s_t)) == (4, 12, 8) and fill * nch == len(g_t) - len(z_t)

    for task in (v_t[0], e_t[0], v_t[1], e_t[1], v_t[2], e_t[2], v_t[3]):
        task()
    dt = _softplus(_dot_nt(wdtT_ref[...], xm) + dtb_ref[...])
    dt_ref[...] = dt
    a_col = -jnp.exp(alog_ref[...])
    preps = [_ssd_prep(dt[HEADS:, c * CHUNK:(c + 1) * CHUNK], a_col[HEADS:], False) for c in range(nch)]
    assert nch == 2
    for task in (u_t[0], c_t[0], u_t[1], c_t[1], n_t[0], u_t[2], c_t[2], n_t[1], u_t[3], c_t[3]):
        task()
    for i in range(4):
        for task in (z_t[i], c_t[4 + 2 * i], g_t[i], c_t[5 + 2 * i], s_t[2 * i], s_t[2 * i + 1]):
            task()
    rest = g_t[len(z_t):]
    for i, c in enumerate(range(nch - 1, -1, -1)):
        rows = slice(c * CHUNK, (c + 1) * CHUNK)
        y = _ssd_chunk(preps[c], xs_ref[rows, :], xsf_ref[rows, :], bc_ref[rows, 0:GROUPS * STATE],
                       bc_ref[rows, GROUPS * STATE:], hb_ref, fwd=False, fillers=rest[i * fill:(i + 1) * fill])
        yb_ref[rows, :] = y.astype(bf16)


def _pass1_call(x, mod3, ln0g, ln0b, w_zx, w_uvg, w_dtT, convw, convb, dtb_col, alog_col, s_b, gmg, gmb, ws, bsx,
                bg, wgp, t):
    bsz, seq, _ = x.shape
    nt = seq // t
    hb = t // HALO
    nhb = seq // HALO
    const = lambda shape: pl.BlockSpec(shape, lambda b, s: (0,) * len(shape), pipeline_mode=pl.Buffered(1))
    tok = lambda w: pl.BlockSpec((None, t, w), lambda b, s: (b, nt - 1 - s, 0))
    act = lambda w, dt_: jax.ShapeDtypeStruct((bsz, seq, w), dt_)
    return pl.pallas_call(
        _pass1_kernel,
        grid=(bsz, nt),
        in_specs=[tok(D),
                  pl.BlockSpec((None, HALO, D), lambda b, s: (b, jnp.maximum((nt - 1 - s) * hb - 1, 0), 0)),
                  pl.BlockSpec((None, HALO, D), lambda b, s: (b, jnp.minimum((nt - s) * hb, nhb - 1), 0)),
                  pl.BlockSpec((None, 1, 2 * D), lambda b, s: (b, 0, 0)),
                  const((1, D)), const((1, D)), const((D, D + XBC)), const((D, 4 * D)), const((2 * HEADS, D)),
                  const((CONV, XBC)), const((1, XBC)), const((2 * HEADS, 1)), const((2 * HEADS, 1)),
                  pl.BlockSpec((None, GROUPS, STATE, GROUP_W), lambda b, s: (b, 0, 0, 0)),
                  const((1, D)), const((1, D)), const((GM_GROUPS, CHUNK, CHUNK)), const((CHUNK, D)),
                  const((1, 2 * D)), const((D, D))],
        out_specs=[tok(D), tok(sum(PACKED)),
                   pl.BlockSpec((None, 2 * HEADS, t), lambda b, s: (b, 0, nt - 1 - s))],
        out_shape=[act(D, f32), act(sum(PACKED), bf16), jax.ShapeDtypeStruct((bsz, 2 * HEADS, seq), f32)],
        scratch_shapes=[pltpu.VMEM((t + 2 * HALO, XBC), f32), pltpu.VMEM((GROUPS, STATE, GROUP_W), f32),
                        pltpu.VMEM((t, D), f32), pltpu.VMEM((t, D), bf16), pltpu.VMEM((t, D), f32),
                        pltpu.VMEM((t, D), bf16), pltpu.VMEM((t, D), bf16), pltpu.VMEM((t, D), f32)],
        compiler_params=pltpu.CompilerParams(dimension_semantics=("arbitrary", "arbitrary"),
                                             vmem_limit_bytes=VMEM_LIMIT),
        name="pass1_bwd",
    )(x, x, x, mod3, ln0g, ln0b, w_zx, w_uvg, w_dtT, convw, convb, dtb_col, alog_col, s_b, gmg, gmb, ws, bsx, bg,
      wgp)


def _pass2_kernel(nt, x0a_ref, moda_ref, modb_ref, pk_ref, dt_ref,
                  alog_ref, dsk_ref, ng_ref, wsp_ref, wo_ref, ln1g_ref, ln1b_ref, sf_ref,
                  w1_ref, w3_ref, w2_ref, ln2g_ref, ln2b_ref,
                  o_ref, hf_ref, ys_ref, x1_ref, hm_ref, a_ref, ff_ref):
    sz_ref, xs_ref, bc_ref, yb_ref, mg_ref, gate_ref = _unpack(pk_ref)
    t = x0a_ref.shape[0]
    nch = t // CHUNK
    step = pl.program_id(0)

    @pl.when(step == 0)
    def _():
        x1_ref[...] = jnp.zeros(x1_ref.shape, f32)

    @pl.when(step % nt == 0)
    def _():
        hf_ref[...] = sf_ref[...]

    hm_ref[...] = (x1_ref[...] * (1.0 + modb_ref[:, 4 * D:5 * D]) + modb_ref[:, 3 * D:4 * D]).astype(bf16)

    def up_task(n0):
        def run():
            cols = slice(n0, n0 + NBLK)
            a_ref[:, cols] = (_silu(_dot(hm_ref[...], w1_ref[:, cols])) * _dot(hm_ref[...], w3_ref[:, cols])).astype(bf16)
        return run

    def down_task(c0):
        def run():
            ff_ref[:, c0:c0 + NBLK] = _dot(a_ref[...], w2_ref[:, c0:c0 + NBLK])
        return run

    tasks = [up_task(n0) for n0 in range(0, D_FF, NBLK)] + [down_task(c0) for c0 in range(0, D, NBLK)]
    per = -(-len(tasks) // nch)

    a_col = -jnp.exp(alog_ref[...])
    dt = dt_ref[...]
    preps = [_ssd_prep(dt[0:HEADS, c * CHUNK:(c + 1) * CHUNK], a_col[0:HEADS], True) for c in range(nch)]
    dsk = dsk_ref[0:1, :] + dsk_ref[1:2, :]
    for c in range(nch):
        rows = slice(c * CHUNK, (c + 1) * CHUNK)
        xs_bf = xs_ref[rows, :]
        xs_f = xs_bf.astype(f32)
        y = _ssd_chunk(preps[c], xs_bf, xs_f, bc_ref[rows, 0:GROUPS * STATE], bc_ref[rows, GROUPS * STATE:],
                       hf_ref, fwd=True, fillers=tasks[c * per:(c + 1) * per])
        y = y + yb_ref[rows, :].astype(f32) + xs_f * dsk
        hh = y * sz_ref[rows, :].astype(f32)
        hh = hh * lax.rsqrt(jnp.mean(hh * hh, axis=-1, keepdims=True) + LN_EPS) * ng_ref[...]
        ys_ref[rows, :] = hh.astype(bf16)

    o_ref[...] = _ln(ALPHA * x1_ref[...] + modb_ref[:, 5 * D:6 * D] * ff_ref[...], ln2g_ref[...], ln2b_ref[...])

    merged = gate_ref[...].astype(f32) * _dot(ys_ref[...], wsp_ref[...]) + mg_ref[...].astype(f32)
    out_x = _dot(merged.astype(bf16), wo_ref[...])
    x1_ref[...] = _ln(x0a_ref[...] + moda_ref[:, 2 * D:3 * D] * out_x, ln1g_ref[...], ln1b_ref[...])


def _pass2_call(x0a, mod3, packed, dt, alog_col, dsk, ng, wsp, wo, ln1g, ln1b, s_f, w1, w3, w2, ln2g, ln2b, t):
    bsz, seq, _ = x0a.shape
    nt = seq // t
    last = bsz * nt - 1
    cur = lambda s: jnp.minimum(s, last)
    prev = lambda s: jnp.maximum(s - 1, 0)
    const = lambda shape: pl.BlockSpec(shape, lambda s: (0,) * len(shape), pipeline_mode=pl.Buffered(1))
    tok = lambda w: pl.BlockSpec((None, t, w), lambda s: (cur(s) // nt, cur(s) % nt, 0))
    return pl.pallas_call(
        functools.partial(_pass2_kernel, nt),
        grid=(bsz * nt + 1,),
        in_specs=[tok(D), pl.BlockSpec((None, 1, D_MOD), lambda s: (cur(s) // nt, 0, 0)),
                  pl.BlockSpec((None, 1, D_MOD), lambda s: (prev(s) // nt, 0, 0)),
                  tok(sum(PACKED)),
                  pl.BlockSpec((None, 2 * HEADS, t), lambda s: (cur(s) // nt, 0, cur(s) % nt)),
                  const((2 * HEADS, 1)), const((2, D)), const((1, D)),
                  const((D, D)), const((D, D)), const((1, D)), const((1, D)),
                  pl.BlockSpec((None, GROUPS, STATE, GROUP_W), lambda s: (cur(s) // nt, 0, 0, 0)),
                  const((D, D_FF)), const((D, D_FF)), const((D_FF, D)), const((1, D)), const((1, D))],
        out_specs=pl.BlockSpec((None, t, D), lambda s: (prev(s) // nt, prev(s) % nt, 0)),
        out_shape=jax.ShapeDtypeStruct((bsz, seq, D), f32),
        scratch_shapes=[pltpu.VMEM((GROUPS, STATE, GROUP_W), f32), pltpu.VMEM((t, D), bf16),
                        pltpu.VMEM((t, D), f32), pltpu.VMEM((t, D), bf16),
                        pltpu.VMEM((t, D_FF), bf16), pltpu.VMEM((t, D), f32)],
        compiler_params=pltpu.CompilerParams(dimension_semantics=("arbitrary",), vmem_limit_bytes=VMEM_LIMIT),
        name="pass2_ffn",
    )(x0a, mod3, mod3, packed, dt, alog_col, dsk, ng, wsp, wo, ln1g, ln1b, s_f, w1, w3, w2, ln2g, ln2b)


def kernel(x, c, ctx, c_ctx, ln0_g, ln0_b, w_ada, b_ada, w_in, conv_w, conv_b, dt_bias, a_log, d_skip, ssd_norm_g, gm_norm_g, gm_norm_b, w_spatial, b_spatial, b_gate, w_ssd_proj, w_gm_proj, w_out, ln1_g, ln1_b, w_ff1, w_ff3, w_ff2, ln2_g, ln2_b):
    bsz, seq, _ = x.shape
    assert x.shape[2] == D and w_in.shape == (DEPTH, D, D_PROJ) and bsz < 16
    assert seq % 512 == 0 and ctx.shape[1] % CHUNK == 0
    row = lambda a: a.reshape(1, -1)

    w_in0 = w_in[0]
    o_dt = D + XBC
    w_zx = w_in0[:, :o_dt].astype(bf16)
    w_uvg = w_in0[:, o_dt + 2 * HEADS:].astype(bf16)
    w_dtT = w_in0[:, o_dt:o_dt + 2 * HEADS].T.astype(bf16)
    dtb_col = dt_bias[0].reshape(2 * HEADS, 1)
    alog_col = a_log[0].reshape(2 * HEADS, 1)
    dsk = jnp.repeat(d_skip[0], HEAD_DIM, axis=1)
    bsx = jnp.repeat(b_spatial[0].T, GM_GROUP_DIM, axis=1)
    ln0g, ln0b = row(ln0_g), row(ln0_b)

    cc = jnp.concatenate([c, c_ctx[None, :], jnp.zeros((15 - bsz, D), f32)], axis=0)
    mod = _mod_call(cc, w_ada[0], row(b_ada[0]))
    mod3 = mod.reshape(16, 1, D_MOD)
    s_f, s_b = _ctx_call(ctx, mod3[bsz:bsz + 1], ln0g, ln0b, w_zx, w_dtT,
                         conv_w[0], row(conv_b[0]), dtb_col, alog_col)

    t = 256
    x0a, packed, dt = _pass1_call(
        x, mod3, ln0g, ln0b, w_zx, w_uvg, w_dtT, conv_w[0], row(conv_b[0]), dtb_col, alog_col, s_b,
        row(gm_norm_g[0]), row(gm_norm_b[0]), w_spatial[0].astype(bf16), bsx, row(b_gate[0]),
        w_gm_proj[0].astype(bf16), t)
    return _pass2_call(x0a, mod3, packed, dt, alog_col, dsk, row(ssd_norm_g[0]),
                       w_ssd_proj[0].astype(bf16), w_out[0].astype(bf16),
                       row(ln1_g[0]), row(ln1_b[0]), s_f,
                       w_ff1[0].astype(bf16), w_ff3[0].astype(bf16), w_ff2[0].astype(bf16),
                       row(ln2_g[0]), row(ln2_b[0]), 2 * t)
```

```python
import functools

import jax
import jax.numpy as jnp
from jax import lax
from jax.experimental import pallas as pl
from jax.experimental.pallas import tpu as pltpu

f32 = jnp.float32
bf16 = jnp.bfloat16

D = 1024
HEADS = 16
HEAD_DIM = 64
GROUPS = 2
GROUP_W = D // GROUPS
STATE = 128
CONV = 5
CHUNK = 128
XBC = D + 2 * GROUPS * STATE
GM_GROUPS = 8
GM_GROUP_DIM = D // GM_GROUPS
D_FF = 2816
D_MOD = 6 * D
DEPTH = 1
ALPHA = (2 * DEPTH) ** 0.25
LN_EPS = 1e-5
HALO = 8
LANES = 128
NBLK = 256

D_PROJ = D + XBC + 2 * HEADS + 4 * D

VMEM_LIMIT = 56 * 1024 * 1024


def _dot(a, b):
    return jnp.dot(a, b, preferred_element_type=f32)


def _dot_nt(a, b):
    return lax.dot_general(a, b, (((1,), (1,)), ((), ())), preferred_element_type=f32)


def _dot_tn(a, b):
    return lax.dot_general(a, b, (((0,), (0,)), ((), ())), preferred_element_type=f32)


def _norm_rows(xf):
    mu = jnp.mean(xf, axis=-1, keepdims=True)
    xc = xf - mu
    return xc * lax.rsqrt(jnp.mean(xc * xc, axis=-1, keepdims=True) + LN_EPS)


def _ln(xf, g, b):
    return _norm_rows(xf) * g + b


def _sigmoid(x):
    return 0.5 * jnp.tanh(0.5 * x) + 0.5


def _silu(x):
    h = 0.5 * x
    return h * jnp.tanh(h) + h


def _softplus(x):
    return jnp.maximum(x, 0.0) + jnp.log1p(jnp.exp(-jnp.abs(x)))


def _scan_rows(a, tri):
    hi = a.astype(bf16)
    r1 = a - hi.astype(f32)
    mid = r1.astype(bf16)
    lo = (r1 - mid.astype(f32)).astype(bf16)
    cs = _dot(jnp.concatenate([hi, mid, lo], axis=0), tri)
    return cs[0:16] + cs[16:32] + cs[32:48]


def _dir_rows(dt, a_col, fwd):
    k = lax.broadcasted_iota(jnp.int32, (CHUNK, CHUNK), 0)
    j = lax.broadcasted_iota(jnp.int32, (CHUNK, CHUNK), 1)
    tri = jnp.where((k <= j) if fwd else (k >= j), 1.0, 0.0).astype(bf16)
    cum = _scan_rows(dt * a_col, tri)
    total = cum[:, CHUNK - 1:CHUNK] if fwd else cum[:, 0:1]
    return cum, jnp.exp(total - cum) * dt, jnp.exp(cum)


def _to_cols(rows):
    pad = jnp.zeros((LANES - 16 * len(rows), CHUNK), f32)
    return jnp.concatenate(list(rows) + [pad], axis=0).T


def _expand_heads(m):
    lane = lax.broadcasted_iota(jnp.int32, (CHUNK, LANES), 1)
    pieces = []
    for p in range(HEADS // 2):
        pieces.append(jnp.where(lane < HEAD_DIM, m[:, 2 * p:2 * p + 1], m[:, 2 * p + 1:2 * p + 2]))
    return jnp.concatenate(pieces, axis=1)


def _ssd_prep(dt, a_col, fwd):
    cum, wdt, ecum = _dir_rows(dt, a_col, fwd)
    return cum - jnp.log(dt), _to_cols([cum, wdt, ecum])


def _ssd_chunk(prep, xs_bf, xs_f, bm, cm, h_ref, fwd, want_y=True, fillers=()):
    cumj, cols = prep
    fillers = list(fillers)
    ecum_x = _expand_heads(cols[:, 32:48])
    y = None
    if want_y:
        ii = lax.broadcasted_iota(jnp.int32, (CHUNK, CHUNK), 0)
        jj = lax.broadcasted_iota(jnp.int32, (CHUNK, CHUNK), 1)
        keep = (ii >= jj) if fwd else (ii <= jj)
        lane = lax.broadcasted_iota(jnp.int32, (CHUNK, LANES), 1)
        cb = [_dot_nt(cm[:, g * STATE:(g + 1) * STATE], bm[:, g * STATE:(g + 1) * STATE]) for g in range(GROUPS)]
        pieces = []
        for p in range(HEADS // 2):
            g = (2 * p) // (HEADS // GROUPS)
            ws = []
            for h in (2 * p, 2 * p + 1):
                seg = cols[:, h:h + 1] - cumj[h:h + 1, :]
                ws.append(jnp.where(keep, cb[g] * jnp.exp(seg), 0.0))
            lhs = jnp.concatenate(ws, axis=1).astype(bf16)
            x2 = xs_bf[:, p * LANES:(p + 1) * LANES]
            zero = jnp.zeros_like(x2)
            rhs = jnp.concatenate([jnp.where(lane < HEAD_DIM, x2, zero),
                                   jnp.where(lane >= HEAD_DIM, x2, zero)], axis=0)
            pieces.append(_dot(lhs, rhs))
            if fillers:
                fillers.pop(0)()
        y_diag = jnp.concatenate(pieces, axis=1)
        y_off = jnp.concatenate([_dot(cm[:, g * STATE:(g + 1) * STATE], h_ref[g].astype(bf16))
                                 for g in range(GROUPS)], axis=1)
        y = y_diag + y_off * ecum_x
    xw = (xs_f * _expand_heads(cols[:, 16:32])).astype(bf16)
    dec = ecum_x[CHUNK - 1:CHUNK, :] if fwd else ecum_x[0:1, :]
    for g in range(GROUPS):
        st = _dot_tn(bm[:, g * STATE:(g + 1) * STATE], xw[:, g * GROUP_W:(g + 1) * GROUP_W])
        h_ref[g] = dec[:, g * GROUP_W:(g + 1) * GROUP_W] * h_ref[g] + st
    for f in fillers:
        f()
    return y


def _conv_silu(ext_ref, convw_ref, convb_ref, t, c0, c1):
    n = (t + 2 * HALO) // 8
    w = c1 - c0
    p = ext_ref[:, c0:c1].reshape(n, 8, w)
    sub = lax.broadcasted_iota(jnp.int32, (1, 8, w), 1)
    mid = CONV // 2
    acc = convb_ref[:, c0:c1].reshape(1, 1, w) + p[1:n - 1] * convw_ref[mid:mid + 1, c0:c1].reshape(1, 1, w)
    for k in range(CONV):
        s = k - mid
        if s > 0:
            mixed = jnp.where(sub >= s, p[1:n - 1], p[2:n])
            shifted = pltpu.roll(mixed, 8 - s, axis=1)
        elif s < 0:
            mixed = jnp.where(sub < 8 + s, p[1:n - 1], p[0:n - 2])
            shifted = pltpu.roll(mixed, -s, axis=1)
        else:
            continue
        acc = acc + shifted * convw_ref[k:k + 1, c0:c1].reshape(1, 1, w)
    return _silu(acc).reshape(t, w)


def _cast_kernel(*refs):
    n = len(refs) // 2
    for src, dst in zip(refs[:n], refs[n:]):
        dst[...] = src[...].astype(bf16)


def _cast_call(*ws):
    rows, cols = ws[0].shape
    rb = 256
    spec = pl.BlockSpec((rb, cols), lambda i: (i, 0))
    return pl.pallas_call(
        _cast_kernel,
        grid=(rows // rb,),
        in_specs=[spec] * len(ws),
        out_specs=[spec] * len(ws),
        out_shape=[jax.ShapeDtypeStruct((rows, cols), bf16)] * len(ws),
        compiler_params=pltpu.CompilerParams(dimension_semantics=("arbitrary",), vmem_limit_bytes=VMEM_LIMIT),
        name="cast_weights",
    )(*ws)


def _split_in_proj_kernel(w_ref, zx_ref, uvg_ref, dt_ref):
    o_dt = D + XBC
    zx_ref[...] = w_ref[:, 0:o_dt].astype(bf16)
    dt_ref[...] = w_ref[:, o_dt:o_dt + 2 * HEADS].astype(bf16)
    uvg_ref[...] = w_ref[:, o_dt + 2 * HEADS:].astype(bf16)


def _split_in_proj_call(w):
    rows = w.shape[0]
    rb = 256
    return pl.pallas_call(
        _split_in_proj_kernel,
        grid=(rows // rb,),
        in_specs=[pl.BlockSpec((rb, D_PROJ), lambda i: (i, 0))],
        out_specs=[pl.BlockSpec((rb, D + XBC), lambda i: (i, 0)), pl.BlockSpec((rb, 4 * D), lambda i: (i, 0)),
                   pl.BlockSpec((rb, 2 * HEADS), lambda i: (i, 0))],
        out_shape=[jax.ShapeDtypeStruct((rows, D + XBC), bf16), jax.ShapeDtypeStruct((rows, 4 * D), bf16),
                   jax.ShapeDtypeStruct((rows, 2 * HEADS), bf16)],
        compiler_params=pltpu.CompilerParams(dimension_semantics=("arbitrary",), vmem_limit_bytes=VMEM_LIMIT),
        name="split_in_proj",
    )(w)


def _mod_kernel(c_ref, w_ref, b_ref, o_ref):
    s = _silu(c_ref[...]).astype(bf16)
    o_ref[...] = _dot(s, w_ref[...].astype(bf16)) + b_ref[...]


def _mod_call(cc, w_ada, b_ada):
    nb = 1024
    return pl.pallas_call(
        _mod_kernel,
        grid=(D_MOD // nb,),
        in_specs=[pl.BlockSpec((16, D), lambda n: (0, 0)),
                  pl.BlockSpec((D, nb), lambda n: (0, n)),
                  pl.BlockSpec((1, nb), lambda n: (0, n))],
        out_specs=pl.BlockSpec((16, nb), lambda n: (0, n)),
        out_shape=jax.ShapeDtypeStruct((16, D_MOD), f32),
        compiler_params=pltpu.CompilerParams(dimension_semantics=("arbitrary",), vmem_limit_bytes=VMEM_LIMIT),
        name="adaln_mod",
    )(cc, w_ada, b_ada)


def _ctx_kernel(ctx_ref, mod_ref, ln0g_ref, ln0b_ref, wzx_ref, wdtT_ref, convw_ref, convb_ref, dtb_ref,
                alog_ref, sf_ref, sb_ref, ext_ref, xs_ref, bc_ref):
    cl = ctx_ref.shape[0]
    sh1 = mod_ref[:, 0:D]
    sc1 = mod_ref[:, D:2 * D]
    xm = (_ln(ctx_ref[...], ln0g_ref[...], ln0b_ref[...]) * (1.0 + sc1) + sh1).astype(bf16)
    ext_ref[pl.ds(0, HALO), :] = jnp.zeros((HALO, XBC), f32)
    ext_ref[pl.ds(HALO + cl, HALO), :] = jnp.zeros((HALO, XBC), f32)
    ext_ref[pl.ds(HALO, cl), :] = _dot(xm, wzx_ref[:, D:])
    for c0 in range(0, D, NBLK):
        xs_ref[:, c0:c0 + NBLK] = _conv_silu(ext_ref, convw_ref, convb_ref, cl, c0, c0 + NBLK)
    bc_ref[...] = _conv_silu(ext_ref, convw_ref, convb_ref, cl, D, XBC).astype(bf16)
    dt = _softplus(_dot_nt(wdtT_ref[...], xm) + dtb_ref[...])
    a_col = -jnp.exp(alog_ref[...])
    sf_ref[...] = jnp.zeros(sf_ref.shape, f32)
    sb_ref[...] = jnp.zeros(sb_ref.shape, f32)
    nch = cl // CHUNK
    jobs = [(fwd, h_ref, c, 0 if fwd else HEADS)
            for k in range(nch) for fwd, h_ref, c in ((True, sf_ref, k), (False, sb_ref, nch - 1 - k))]
    preps = [_ssd_prep(dt[r0:r0 + HEADS, c * CHUNK:(c + 1) * CHUNK], a_col[r0:r0 + HEADS], fwd)
             for fwd, _, c, r0 in jobs]
    for prep, (fwd, h_ref, c, _) in zip(preps, jobs):
        rows = slice(c * CHUNK, (c + 1) * CHUNK)
        _ssd_chunk(prep, None, xs_ref[rows, :], bc_ref[rows, 0:GROUPS * STATE], None, h_ref, fwd, want_y=False)


def _ctx_call(ctx, mod3, ln0g, ln0b, w_zx, w_dtT, convw, convb, dtb_col, alog_col):
    bsz, cl, _ = ctx.shape
    full = lambda shape: pl.BlockSpec(shape, lambda b: (0,) * len(shape))
    st_shape = jax.ShapeDtypeStruct((bsz, GROUPS, STATE, GROUP_W), f32)
    st_spec = pl.BlockSpec((None, GROUPS, STATE, GROUP_W), lambda b: (b, 0, 0, 0))
    return pl.pallas_call(
        _ctx_kernel,
        grid=(bsz,),
        in_specs=[pl.BlockSpec((None, cl, D), lambda b: (b, 0, 0)),
                  pl.BlockSpec((None, 1, 2 * D), lambda b: (0, 0, 0)),
                  full((1, D)), full((1, D)), full((D, D + XBC)), full((2 * HEADS, D)),
                  full((CONV, XBC)), full((1, XBC)), full((2 * HEADS, 1)), full((2 * HEADS, 1))],
        out_specs=[st_spec, st_spec],
        out_shape=[st_shape, st_shape],
        scratch_shapes=[pltpu.VMEM((cl + 2 * HALO, XBC), f32), pltpu.VMEM((cl, D), f32),
                        pltpu.VMEM((cl, 2 * GROUPS * STATE), bf16)],
        compiler_params=pltpu.CompilerParams(dimension_semantics=("arbitrary",), vmem_limit_bytes=VMEM_LIMIT),
        name="ctx_states",
    )(ctx, mod3, ln0g, ln0b, w_zx, w_dtT, convw, convb, dtb_col, alog_col)


def _pass1_kernel(x_ref, xp_ref, xn_ref, mod_ref, ln0g_ref, ln0b_ref, wzx_ref, wuvg_ref, wdtT_ref, convw_ref,
                  convb_ref, dtb_ref, alog_ref, sb_ref, gmg_ref, gmb_ref, ws_ref, bsx_ref, bg_ref, wgp_ref,
                  x0a_ref, sz_ref, xs_ref, bc_ref, yb_ref, mg_ref, gate_ref, dt_ref,
                  ext_ref, hb_ref, xsf_ref, ug_ref, vg_ref, vn_ref, ygm_ref, gg_ref):
    t = x_ref.shape[0]
    nch = t // CHUNK
    step = pl.program_id(1)
    nt = pl.num_programs(1)
    tile = nt - 1 - step

    @pl.when(step == 0)
    def _():
        hb_ref[...] = sb_ref[...]

    g0 = ln0g_ref[...]
    b0 = ln0b_ref[...]
    sc1p = 1.0 + mod_ref[:, D:2 * D]
    gm = g0 * sc1p
    bm = b0 * sc1p + mod_ref[:, 0:D]

    xn = _norm_rows(x_ref[...])
    x0a_ref[...] = xn * (ALPHA * g0) + ALPHA * b0
    xm_f = xn * gm + bm
    xm = xm_f.astype(bf16)
    ext = jnp.concatenate([_norm_rows(xp_ref[...]) * gm + bm, xm_f, _norm_rows(xn_ref[...]) * gm + bm],
                          axis=0).astype(bf16)
    row = lax.broadcasted_iota(jnp.int32, (t + 2 * HALO, 1), 0)
    valid = jnp.logical_and(jnp.logical_or(row >= HALO, tile > 0), jnp.logical_or(row < HALO + t, tile < nt - 1))

    def ext_task(c0):
        def run():
            w = XBC // 3
            ext_ref[:, c0:c0 + w] = jnp.where(valid, _dot(ext, wzx_ref[:, D + c0:D + c0 + w]), 0.0)
        return run

    def proj_task(w_ref, w0, c0, dst_ref, fn):
        def run():
            cols = slice(c0, c0 + NBLK)
            dst_ref[:, cols] = fn(_dot(xm, w_ref[:, w0 + c0:w0 + c0 + NBLK]), cols).astype(dst_ref.dtype)
        return run

    def conv_task(c0):
        def run():
            blk = _conv_silu(ext_ref, convw_ref, convb_ref, t, c0, c0 + LANES)
            if c0 < D:
                xsf_ref[:, c0:c0 + LANES] = blk
                xs_ref[:, c0:c0 + LANES] = blk.astype(bf16)
            else:
                bc_ref[:, c0 - D:c0 - D + LANES] = blk.astype(bf16)
        return run

    def vnorm_task(c):
        def run():
            rows = slice(c * CHUNK, (c + 1) * CHUNK)
            vn_ref[rows, :] = _ln(vg_ref[rows, :], gmg_ref[...], gmb_ref[...]).astype(bf16)
        return run

    def gating_task(g):
        def run():
            cols = slice(g * GM_GROUP_DIM, (g + 1) * GM_GROUP_DIM)
            rhs = jnp.concatenate([vn_ref[c * CHUNK:(c + 1) * CHUNK, cols] for c in range(nch)], axis=1)
            mixed = _dot(ws_ref[g], rhs)
            for c in range(nch):
                rows = slice(c * CHUNK, (c + 1) * CHUNK)
                ygm_ref[rows, cols] = (ug_ref[rows, cols].astype(f32)
                                       * (mixed[:, c * GM_GROUP_DIM:(c + 1) * GM_GROUP_DIM]
                                          + bsx_ref[:, cols])).astype(bf16)
        return run

    blocks = range(0, D, NBLK)
    gelu = lambda r, cols: jax.nn.gelu(r)
    u_t = [proj_task(wuvg_ref, 0, c0, ug_ref, gelu) for c0 in blocks]
    v_t = [proj_task(wuvg_ref, D, c0, vg_ref, gelu) for c0 in blocks]
    z_t = [proj_task(wzx_ref, 0, c0, sz_ref, lambda r, cols: _silu(r)) for c0 in blocks]
    gs_t = [proj_task(wuvg_ref, 2 * D, c0, gate_ref, lambda r, cols: _sigmoid(r + bg_ref[:, cols]))
            for c0 in blocks]
    gg_t = [proj_task(wuvg_ref, 3 * D, c0, gg_ref,
                      lambda r, cols: _sigmoid(r + bg_ref[:, D + cols.start:D + cols.stop])) for c0 in blocks]

    def gm_task(c0):
        def run():
            cols = slice(c0, c0 + NBLK)
            mg_ref[:, cols] = (gg_ref[:, cols] * _dot(ygm_ref[...], wgp_ref[:, cols])).astype(bf16)
        return run

    gm_t = [gm_task(c0) for c0 in blocks]
    e_t = [ext_task(c0) for c0 in range(0, XBC, XBC // 3)]
    c_t = [conv_task(c0) for c0 in range(0, XBC, LANES)]
    n_t = [vnorm_task(c) for c in range(nch)]
    s_t = [gating_task(g) for g in range(GM_GROUPS)]
    g_t = gg_t + [t_ for pair in zip(gs_t, gm_t) for t_ in pair]
    fill = (len(g_t) - len(z_t)) // nch
    assert (len(v_t), len(c_t), len(s_t)) == (4, 12, 8) and fill * nch == len(g_t) - len(z_t)

    for task in (v_t[0], e_t[0], v_t[1], e_t[1], v_t[2], e_t[2], v_t[3]):
        task()
    dt = _softplus(_dot_nt(wdtT_ref[...], xm) + dtb_ref[...])
    dt_ref[...] = dt
    a_col = -jnp.exp(alog_ref[...])
    preps = [_ssd_prep(dt[HEADS:, c * CHUNK:(c + 1) * CHUNK], a_col[HEADS:], False) for c in range(nch)]
    assert nch == 2
    for task in (u_t[0], c_t[0], u_t[1], c_t[1], n_t[0], u_t[2], c_t[2], n_t[1], u_t[3], c_t[3]):
        task()
    for i in range(4):
        for task in (z_t[i], c_t[4 + 2 * i], g_t[i], c_t[5 + 2 * i], s_t[2 * i], s_t[2 * i + 1]):
            task()
    rest = g_t[len(z_t):]
    for i, c in enumerate(range(nch - 1, -1, -1)):
        rows = slice(c * CHUNK, (c + 1) * CHUNK)
        y = _ssd_chunk(preps[c], xs_ref[rows, :], xsf_ref[rows, :], bc_ref[rows, 0:GROUPS * STATE],
                       bc_ref[rows, GROUPS * STATE:], hb_ref, fwd=False, fillers=rest[i * fill:(i + 1) * fill])
        yb_ref[rows, :] = y.astype(bf16)


def _pass1_call(x, mod3, ln0g, ln0b, w_zx, w_uvg, w_dtT, convw, convb, dtb_col, alog_col, s_b, gmg, gmb, ws, bsx,
                bg, wgp, t):
    bsz, seq, _ = x.shape
    nt = seq // t
    hb = t // HALO
    nhb = seq // HALO
    const = lambda shape: pl.BlockSpec(shape, lambda b, s: (0,) * len(shape), pipeline_mode=pl.Buffered(1))
    tok = lambda w: pl.BlockSpec((None, t, w), lambda b, s: (b, nt - 1 - s, 0))
    act = lambda w, dt_: jax.ShapeDtypeStruct((bsz, seq, w), dt_)
    return pl.pallas_call(
        _pass1_kernel,
        grid=(bsz, nt),
        in_specs=[tok(D),
                  pl.BlockSpec((None, HALO, D), lambda b, s: (b, jnp.maximum((nt - 1 - s) * hb - 1, 0), 0)),
                  pl.BlockSpec((None, HALO, D), lambda b, s: (b, jnp.minimum((nt - s) * hb, nhb - 1), 0)),
                  pl.BlockSpec((None, 1, 2 * D), lambda b, s: (b, 0, 0)),
                  const((1, D)), const((1, D)), const((D, D + XBC)), const((D, 4 * D)), const((2 * HEADS, D)),
                  const((CONV, XBC)), const((1, XBC)), const((2 * HEADS, 1)), const((2 * HEADS, 1)),
                  pl.BlockSpec((None, GROUPS, STATE, GROUP_W), lambda b, s: (b, 0, 0, 0)),
                  const((1, D)), const((1, D)), const((GM_GROUPS, CHUNK, CHUNK)), const((CHUNK, D)),
                  const((1, 2 * D)), const((D, D))],
        out_specs=[tok(D), tok(D), tok(D), tok(2 * GROUPS * STATE), tok(D), tok(D), tok(D),
                   pl.BlockSpec((None, 2 * HEADS, t), lambda b, s: (b, 0, nt - 1 - s))],
        out_shape=[act(D, f32), act(D, bf16), act(D, bf16), act(2 * GROUPS * STATE, bf16), act(D, bf16),
                   act(D, bf16), act(D, bf16), jax.ShapeDtypeStruct((bsz, 2 * HEADS, seq), f32)],
        scratch_shapes=[pltpu.VMEM((t + 2 * HALO, XBC), f32), pltpu.VMEM((GROUPS, STATE, GROUP_W), f32),
                        pltpu.VMEM((t, D), f32), pltpu.VMEM((t, D), bf16), pltpu.VMEM((t, D), f32),
                        pltpu.VMEM((t, D), bf16), pltpu.VMEM((t, D), bf16), pltpu.VMEM((t, D), f32)],
        compiler_params=pltpu.CompilerParams(dimension_semantics=("arbitrary", "arbitrary"),
                                             vmem_limit_bytes=VMEM_LIMIT),
        name="pass1_bwd",
    )(x, x, x, mod3, ln0g, ln0b, w_zx, w_uvg, w_dtT, convw, convb, dtb_col, alog_col, s_b, gmg, gmb, ws, bsx, bg,
      wgp)


def _pass2_kernel(nt, x0a_ref, moda_ref, modb_ref, sz_ref, xs_ref, bc_ref, yb_ref, mg_ref, gate_ref, dt_ref,
                  alog_ref, dsk_ref, ng_ref, wsp_ref, wo_ref, ln1g_ref, ln1b_ref, sf_ref,
                  w1_ref, w3_ref, w2_ref, ln2g_ref, ln2b_ref,
                  o_ref, hf_ref, ys_ref, x1_ref, hm_ref, a_ref, ff_ref):
    t = x0a_ref.shape[0]
    nch = t // CHUNK
    step = pl.program_id(0)

    @pl.when(step == 0)
    def _():
        x1_ref[...] = jnp.zeros(x1_ref.shape, f32)

    @pl.when(step % nt == 0)
    def _():
        hf_ref[...] = sf_ref[...]

    hm_ref[...] = (x1_ref[...] * (1.0 + modb_ref[:, 4 * D:5 * D]) + modb_ref[:, 3 * D:4 * D]).astype(bf16)

    def up_task(n0):
        def run():
            cols = slice(n0, n0 + NBLK)
            a_ref[:, cols] = (_silu(_dot(hm_ref[...], w1_ref[:, cols])) * _dot(hm_ref[...], w3_ref[:, cols])).astype(bf16)
        return run

    def down_task(c0):
        def run():
            ff_ref[:, c0:c0 + NBLK] = _dot(a_ref[...], w2_ref[:, c0:c0 + NBLK])
        return run

    tasks = [up_task(n0) for n0 in range(0, D_FF, NBLK)] + [down_task(c0) for c0 in range(0, D, NBLK)]
    per = -(-len(tasks) // nch)

    a_col = -jnp.exp(alog_ref[...])
    dt = dt_ref[...]
    preps = [_ssd_prep(dt[0:HEADS, c * CHUNK:(c + 1) * CHUNK], a_col[0:HEADS], True) for c in range(nch)]
    dsk = dsk_ref[0:1, :] + dsk_ref[1:2, :]
    for c in range(nch):
        rows = slice(c * CHUNK, (c + 1) * CHUNK)
        xs_bf = xs_ref[rows, :]
        xs_f = xs_bf.astype(f32)
        y = _ssd_chunk(preps[c], xs_bf, xs_f, bc_ref[rows, 0:GROUPS * STATE], bc_ref[rows, GROUPS * STATE:],
                       hf_ref, fwd=True, fillers=tasks[c * per:(c + 1) * per])
        y = y + yb_ref[rows, :].astype(f32) + xs_f * dsk
        hh = y * sz_ref[rows, :].astype(f32)
        hh = hh * lax.rsqrt(jnp.mean(hh * hh, axis=-1, keepdims=True) + LN_EPS) * ng_ref[...]
        ys_ref[rows, :] = hh.astype(bf16)

    o_ref[...] = _ln(ALPHA * x1_ref[...] + modb_ref[:, 5 * D:6 * D] * ff_ref[...], ln2g_ref[...], ln2b_ref[...])

    merged = gate_ref[...].astype(f32) * _dot(ys_ref[...], wsp_ref[...]) + mg_ref[...].astype(f32)
    out_x = _dot(merged.astype(bf16), wo_ref[...])
    x1_ref[...] = _ln(x0a_ref[...] + moda_ref[:, 2 * D:3 * D] * out_x, ln1g_ref[...], ln1b_ref[...])


def _pass2_call(x0a, mod3, sz, xs, bc, yb, mg, gate, dt, alog_col, dsk, ng, wsp, wo, ln1g, ln1b, s_f,
                w1, w3, w2, ln2g, ln2b, t):
    bsz, seq, _ = x0a.shape
    nt = seq // t
    last = bsz * nt - 1
    cur = lambda s: jnp.minimum(s, last)
    prev = lambda s: jnp.maximum(s - 1, 0)
    const = lambda shape: pl.BlockSpec(shape, lambda s: (0,) * len(shape), pipeline_mode=pl.Buffered(1))
    tok = lambda w: pl.BlockSpec((None, t, w), lambda s: (cur(s) // nt, cur(s) % nt, 0))
    return pl.pallas_call(
        functools.partial(_pass2_kernel, nt),
        grid=(bsz * nt + 1,),
        in_specs=[tok(D), pl.BlockSpec((None, 1, D_MOD), lambda s: (cur(s) // nt, 0, 0)),
                  pl.BlockSpec((None, 1, D_MOD), lambda s: (prev(s) // nt, 0, 0)),
                  tok(D), tok(D), tok(2 * GROUPS * STATE), tok(D), tok(D), tok(D),
                  pl.BlockSpec((None, 2 * HEADS, t), lambda s: (cur(s) // nt, 0, cur(s) % nt)),
                  const((2 * HEADS, 1)), const((2, D)), const((1, D)),
                  const((D, D)), const((D, D)), const((1, D)), const((1, D)),
                  pl.BlockSpec((None, GROUPS, STATE, GROUP_W), lambda s: (cur(s) // nt, 0, 0, 0)),
                  const((D, D_FF)), const((D, D_FF)), const((D_FF, D)), const((1, D)), const((1, D))],
        out_specs=pl.BlockSpec((None, t, D), lambda s: (prev(s) // nt, prev(s) % nt, 0)),
        out_shape=jax.ShapeDtypeStruct((bsz, seq, D), f32),
        scratch_shapes=[pltpu.VMEM((GROUPS, STATE, GROUP_W), f32), pltpu.VMEM((t, D), bf16),
                        pltpu.VMEM((t, D), f32), pltpu.VMEM((t, D), bf16),
                        pltpu.VMEM((t, D_FF), bf16), pltpu.VMEM((t, D), f32)],
        compiler_params=pltpu.CompilerParams(dimension_semantics=("arbitrary",), vmem_limit_bytes=VMEM_LIMIT),
        name="pass2_ffn",
    )(x0a, mod3, mod3, sz, xs, bc, yb, mg, gate, dt, alog_col, dsk, ng, wsp, wo, ln1g, ln1b, s_f,
      w1, w3, w2, ln2g, ln2b)


def kernel(x, c, ctx, c_ctx, ln0_g, ln0_b, w_ada, b_ada, w_in, conv_w, conv_b, dt_bias, a_log, d_skip, ssd_norm_g, gm_norm_g, gm_norm_b, w_spatial, b_spatial, b_gate, w_ssd_proj, w_gm_proj, w_out, ln1_g, ln1_b, w_ff1, w_ff3, w_ff2, ln2_g, ln2_b):
    bsz, seq, _ = x.shape
    assert x.shape[2] == D and w_in.shape == (DEPTH, D, D_PROJ) and bsz < 16
    assert seq % 512 == 0 and ctx.shape[1] % CHUNK == 0
    row = lambda a: a.reshape(1, -1)

    w_zx, w_uvg, w_dt = _split_in_proj_call(w_in[0])
    w_dtT = w_dt.T
    w_ssd, w_gm, w_o = _cast_call(w_ssd_proj[0], w_gm_proj[0], w_out[0])
    w_1, w_3 = _cast_call(w_ff1[0], w_ff3[0])
    (w_2,) = _cast_call(w_ff2[0])
    dtb_col = dt_bias[0].reshape(2 * HEADS, 1)
    alog_col = a_log[0].reshape(2 * HEADS, 1)
    dsk = jnp.repeat(d_skip[0], HEAD_DIM, axis=1)
    bsx = jnp.repeat(b_spatial[0].T, GM_GROUP_DIM, axis=1)
    ln0g, ln0b = row(ln0_g), row(ln0_b)

    cc = jnp.concatenate([c, c_ctx[None, :], jnp.zeros((15 - bsz, D), f32)], axis=0)
    mod = _mod_call(cc, w_ada[0], row(b_ada[0]))
    mod3 = mod.reshape(16, 1, D_MOD)
    s_f, s_b = _ctx_call(ctx, mod3[bsz:bsz + 1], ln0g, ln0b, w_zx, w_dtT,
                         conv_w[0], row(conv_b[0]), dtb_col, alog_col)

    t = 256
    x0a, sz, xs, bc, yb, mg, gate, dt = _pass1_call(
        x, mod3, ln0g, ln0b, w_zx, w_uvg, w_dtT, conv_w[0], row(conv_b[0]), dtb_col, alog_col, s_b,
        row(gm_norm_g[0]), row(gm_norm_b[0]), w_spatial[0].astype(bf16), bsx, row(b_gate[0]), w_gm, t)
    return _pass2_call(x0a, mod3, sz, xs, bc, yb, mg, gate, dt, alog_col, dsk, row(ssd_norm_g[0]), w_ssd, w_o,
                       row(ln1_g[0]), row(ln1_b[0]), s_f, w_1, w_3, w_2, row(ln2_g[0]), row(ln2_b[0]), 2 * t)
```

```python
import functools

import jax
import jax.numpy as jnp
from jax import lax
from jax.experimental import pallas as pl
from jax.experimental.pallas import tpu as pltpu

f32 = jnp.float32
bf16 = jnp.bfloat16

D = 1024
HEADS = 16
HEAD_DIM = 64
GROUPS = 2
GROUP_W = D // GROUPS
STATE = 128
CONV = 5
CHUNK = 128
XBC = D + 2 * GROUPS * STATE
GM_GROUPS = 8
GM_GROUP_DIM = D // GM_GROUPS
D_FF = 2816
D_MOD = 6 * D
DEPTH = 1
ALPHA = (2 * DEPTH) ** 0.25
LN_EPS = 1e-5
HALO = 8
LANES = 128
NBLK = 256

D_PROJ = D + XBC + 2 * HEADS + 4 * D

VMEM_LIMIT = 58 * 1024 * 1024


def _dot(a, b):
    return jnp.dot(a, b, preferred_element_type=f32)


def _dot_nt(a, b):
    return lax.dot_general(a, b, (((1,), (1,)), ((), ())), preferred_element_type=f32)


def _dot_tn(a, b):
    return lax.dot_general(a, b, (((0,), (0,)), ((), ())), preferred_element_type=f32)


def _norm_rows(xf):
    mu = jnp.mean(xf, axis=-1, keepdims=True)
    xc = xf - mu
    return xc * lax.rsqrt(jnp.mean(xc * xc, axis=-1, keepdims=True) + LN_EPS)


def _ln(xf, g, b):
    return _norm_rows(xf) * g + b


def _sigmoid(x):
    return 0.5 * jnp.tanh(0.5 * x) + 0.5


def _silu(x):
    h = 0.5 * x
    return h * jnp.tanh(h) + h


def _softplus(x):
    return jnp.maximum(x, 0.0) + jnp.log1p(jnp.exp(-jnp.abs(x)))


def _scan_rows(a, tri):
    hi = a.astype(bf16)
    r1 = a - hi.astype(f32)
    mid = r1.astype(bf16)
    lo = (r1 - mid.astype(f32)).astype(bf16)
    cs = _dot(jnp.concatenate([hi, mid, lo], axis=0), tri)
    return cs[0:16] + cs[16:32] + cs[32:48]


def _dir_rows(dt, a_col, fwd):
    k = lax.broadcasted_iota(jnp.int32, (CHUNK, CHUNK), 0)
    j = lax.broadcasted_iota(jnp.int32, (CHUNK, CHUNK), 1)
    tri = jnp.where((k <= j) if fwd else (k >= j), 1.0, 0.0).astype(bf16)
    cum = _scan_rows(dt * a_col, tri)
    total = cum[:, CHUNK - 1:CHUNK] if fwd else cum[:, 0:1]
    return cum, jnp.exp(total - cum) * dt, jnp.exp(cum)


def _to_cols(rows):
    pad = jnp.zeros((LANES - 16 * len(rows), CHUNK), f32)
    return jnp.concatenate(list(rows) + [pad], axis=0).T


def _expand_heads(m):
    lane = lax.broadcasted_iota(jnp.int32, (CHUNK, LANES), 1)
    pieces = []
    for p in range(HEADS // 2):
        pieces.append(jnp.where(lane < HEAD_DIM, m[:, 2 * p:2 * p + 1], m[:, 2 * p + 1:2 * p + 2]))
    return jnp.concatenate(pieces, axis=1)


def _ssd_prep(dt, a_col, fwd):
    cum, wdt, ecum = _dir_rows(dt, a_col, fwd)
    return cum - jnp.log(dt), _to_cols([cum, wdt, ecum])


def _ssd_chunk(prep, xs_bf, xs_f, bm, cm, h_ref, fwd, want_y=True, fillers=()):
    cumj, cols = prep
    fillers = list(fillers)
    ecum_x = _expand_heads(cols[:, 32:48])
    y = None
    if want_y:
        ii = lax.broadcasted_iota(jnp.int32, (CHUNK, CHUNK), 0)
        jj = lax.broadcasted_iota(jnp.int32, (CHUNK, CHUNK), 1)
        keep = (ii >= jj) if fwd else (ii <= jj)
        lane = lax.broadcasted_iota(jnp.int32, (CHUNK, LANES), 1)
        cb = [_dot_nt(cm[:, g * STATE:(g + 1) * STATE], bm[:, g * STATE:(g + 1) * STATE]) for g in range(GROUPS)]
        pieces = []
        for p in range(HEADS // 2):
            g = (2 * p) // (HEADS // GROUPS)
            ws = []
            for h in (2 * p, 2 * p + 1):
                seg = cols[:, h:h + 1] - cumj[h:h + 1, :]
                ws.append(jnp.where(keep, cb[g] * jnp.exp(seg), 0.0))
            lhs = jnp.concatenate(ws, axis=1).astype(bf16)
            x2 = xs_bf[:, p * LANES:(p + 1) * LANES]
            zero = jnp.zeros_like(x2)
            rhs = jnp.concatenate([jnp.where(lane < HEAD_DIM, x2, zero),
                                   jnp.where(lane >= HEAD_DIM, x2, zero)], axis=0)
            pieces.append(_dot(lhs, rhs))
            if fillers:
                fillers.pop(0)()
        y_diag = jnp.concatenate(pieces, axis=1)
        y_off = jnp.concatenate([_dot(cm[:, g * STATE:(g + 1) * STATE], h_ref[g].astype(bf16))
                                 for g in range(GROUPS)], axis=1)
        y = y_diag + y_off * ecum_x
    xw = (xs_f * _expand_heads(cols[:, 16:32])).astype(bf16)
    dec = ecum_x[CHUNK - 1:CHUNK, :] if fwd else ecum_x[0:1, :]
    for g in range(GROUPS):
        st = _dot_tn(bm[:, g * STATE:(g + 1) * STATE], xw[:, g * GROUP_W:(g + 1) * GROUP_W])
        h_ref[g] = dec[:, g * GROUP_W:(g + 1) * GROUP_W] * h_ref[g] + st
    for f in fillers:
        f()
    return y


def _conv_silu(ext_ref, convw_ref, convb_ref, t, c0, c1):
    n = (t + 2 * HALO) // 8
    w = c1 - c0
    p = ext_ref[:, c0:c1].reshape(n, 8, w)
    sub = lax.broadcasted_iota(jnp.int32, (1, 8, w), 1)
    mid = CONV // 2
    acc = convb_ref[:, c0:c1].reshape(1, 1, w) + p[1:n - 1] * convw_ref[mid:mid + 1, c0:c1].reshape(1, 1, w)
    for k in range(CONV):
        s = k - mid
        if s > 0:
            mixed = jnp.where(sub >= s, p[1:n - 1], p[2:n])
            shifted = pltpu.roll(mixed, 8 - s, axis=1)
        elif s < 0:
            mixed = jnp.where(sub < 8 + s, p[1:n - 1], p[0:n - 2])
            shifted = pltpu.roll(mixed, -s, axis=1)
        else:
            continue
        acc = acc + shifted * convw_ref[k:k + 1, c0:c1].reshape(1, 1, w)
    return _silu(acc).reshape(t, w)


def _mod_kernel(c_ref, w_ref, b_ref, o_ref):
    s = _silu(c_ref[...]).astype(bf16)
    o_ref[...] = _dot(s, w_ref[...].astype(bf16)) + b_ref[...]


def _mod_call(cc, w_ada, b_ada):
    nb = 1024
    return pl.pallas_call(
        _mod_kernel,
        grid=(D_MOD // nb,),
        in_specs=[pl.BlockSpec((16, D), lambda n: (0, 0)),
                  pl.BlockSpec((D, nb), lambda n: (0, n)),
                  pl.BlockSpec((1, nb), lambda n: (0, n))],
        out_specs=pl.BlockSpec((16, nb), lambda n: (0, n)),
        out_shape=jax.ShapeDtypeStruct((16, D_MOD), f32),
        compiler_params=pltpu.CompilerParams(dimension_semantics=("arbitrary",), vmem_limit_bytes=VMEM_LIMIT),
        name="adaln_mod",
    )(cc, w_ada, b_ada)


def _ctx_kernel(ctx_ref, mod_ref, ln0g_ref, ln0b_ref, wzx_ref, wdtT_ref, convw_ref, convb_ref, dtb_ref,
                alog_ref, sf_ref, sb_ref, ext_ref, xs_ref, bc_ref):
    cl = ctx_ref.shape[0]
    sh1 = mod_ref[:, 0:D]
    sc1 = mod_ref[:, D:2 * D]
    xm = (_ln(ctx_ref[...], ln0g_ref[...], ln0b_ref[...]) * (1.0 + sc1) + sh1).astype(bf16)
    ext_ref[pl.ds(0, HALO), :] = jnp.zeros((HALO, XBC), f32)
    ext_ref[pl.ds(HALO + cl, HALO), :] = jnp.zeros((HALO, XBC), f32)
    ext_ref[pl.ds(HALO, cl), :] = _dot(xm, wzx_ref[:, D:])
    for c0 in range(0, D, NBLK):
        xs_ref[:, c0:c0 + NBLK] = _conv_silu(ext_ref, convw_ref, convb_ref, cl, c0, c0 + NBLK)
    bc_ref[...] = _conv_silu(ext_ref, convw_ref, convb_ref, cl, D, XBC).astype(bf16)
    dt = _softplus(_dot_nt(wdtT_ref[...], xm) + dtb_ref[...])
    a_col = -jnp.exp(alog_ref[...])
    sf_ref[...] = jnp.zeros(sf_ref.shape, f32)
    sb_ref[...] = jnp.zeros(sb_ref.shape, f32)
    nch = cl // CHUNK
    jobs = [(fwd, h_ref, c, 0 if fwd else HEADS)
            for k in range(nch) for fwd, h_ref, c in ((True, sf_ref, k), (False, sb_ref, nch - 1 - k))]
    preps = [_ssd_prep(dt[r0:r0 + HEADS, c * CHUNK:(c + 1) * CHUNK], a_col[r0:r0 + HEADS], fwd)
             for fwd, _, c, r0 in jobs]
    for prep, (fwd, h_ref, c, _) in zip(preps, jobs):
        rows = slice(c * CHUNK, (c + 1) * CHUNK)
        _ssd_chunk(prep, None, xs_ref[rows, :], bc_ref[rows, 0:GROUPS * STATE], None, h_ref, fwd, want_y=False)


def _ctx_call(ctx, mod3, ln0g, ln0b, w_zx, w_dtT, convw, convb, dtb_col, alog_col):
    bsz, cl, _ = ctx.shape
    full = lambda shape: pl.BlockSpec(shape, lambda b: (0,) * len(shape))
    st_shape = jax.ShapeDtypeStruct((bsz, GROUPS, STATE, GROUP_W), f32)
    st_spec = pl.BlockSpec((None, GROUPS, STATE, GROUP_W), lambda b: (b, 0, 0, 0))
    return pl.pallas_call(
        _ctx_kernel,
        grid=(bsz,),
        in_specs=[pl.BlockSpec((None, cl, D), lambda b: (b, 0, 0)),
                  pl.BlockSpec((None, 1, 2 * D), lambda b: (0, 0, 0)),
                  full((1, D)), full((1, D)), full((D, D + XBC)), full((2 * HEADS, D)),
                  full((CONV, XBC)), full((1, XBC)), full((2 * HEADS, 1)), full((2 * HEADS, 1))],
        out_specs=[st_spec, st_spec],
        out_shape=[st_shape, st_shape],
        scratch_shapes=[pltpu.VMEM((cl + 2 * HALO, XBC), f32), pltpu.VMEM((cl, D), f32),
                        pltpu.VMEM((cl, 2 * GROUPS * STATE), bf16)],
        compiler_params=pltpu.CompilerParams(dimension_semantics=("arbitrary",), vmem_limit_bytes=VMEM_LIMIT),
        name="ctx_states",
    )(ctx, mod3, ln0g, ln0b, w_zx, w_dtT, convw, convb, dtb_col, alog_col)


PACKED = (D, D, 2 * GROUPS * STATE, D, D, D)


def _unpack(ref):
    views, start = [], 0
    for width in PACKED:
        views.append(ref.at[:, start:start + width])
        start += width
    return views


def _pass1_kernel(x_ref, xp_ref, xn_ref, mod_ref, ln0g_ref, ln0b_ref, wzx_ref, wuvg_ref, wdtT_ref, convw_ref,
                  convb_ref, dtb_ref, alog_ref, sb_ref, gmg_ref, gmb_ref, ws_ref, bsx_ref, bg_ref, wgp_ref,
                  x0a_ref, pk_ref, dt_ref,
                  ext_ref, hb_ref, xsf_ref, ug_ref, vg_ref, vn_ref, ygm_ref, gg_ref):
    sz_ref, xs_ref, bc_ref, yb_ref, mg_ref, gate_ref = _unpack(pk_ref)
    t = x_ref.shape[0]
    nch = t // CHUNK
    step = pl.program_id(1)
    nt = pl.num_programs(1)
    tile = nt - 1 - step

    @pl.when(step == 0)
    def _():
        hb_ref[...] = sb_ref[...]

    g0 = ln0g_ref[...]
    b0 = ln0b_ref[...]
    sc1p = 1.0 + mod_ref[:, D:2 * D]
    gm = g0 * sc1p
    bm = b0 * sc1p + mod_ref[:, 0:D]

    xn = _norm_rows(x_ref[...])
    x0a_ref[...] = xn * (ALPHA * g0) + ALPHA * b0
    xm_f = xn * gm + bm
    xm = xm_f.astype(bf16)
    ext = jnp.concatenate([_norm_rows(xp_ref[...]) * gm + bm, xm_f, _norm_rows(xn_ref[...]) * gm + bm],
                          axis=0).astype(bf16)
    row = lax.broadcasted_iota(jnp.int32, (t + 2 * HALO, 1), 0)
    valid = jnp.logical_and(jnp.logical_or(row >= HALO, tile > 0), jnp.logical_or(row < HALO + t, tile < nt - 1))

    def ext_task(c0):
        def run():
            w = XBC // 3
            ext_ref[:, c0:c0 + w] = jnp.where(valid, _dot(ext, wzx_ref[:, D + c0:D + c0 + w]), 0.0)
        return run

    def proj_task(w_ref, w0, c0, dst_ref, fn):
        def run():
            cols = slice(c0, c0 + NBLK)
            dst_ref[:, cols] = fn(_dot(xm, w_ref[:, w0 + c0:w0 + c0 + NBLK]), cols).astype(dst_ref.dtype)
        return run

    def conv_task(c0):
        def run():
            blk = _conv_silu(ext_ref, convw_ref, convb_ref, t, c0, c0 + LANES)
            if c0 < D:
                xsf_ref[:, c0:c0 + LANES] = blk
                xs_ref[:, c0:c0 + LANES] = blk.astype(bf16)
            else:
                bc_ref[:, c0 - D:c0 - D + LANES] = blk.astype(bf16)
        return run

    def vnorm_task(c):
        def run():
            rows = slice(c * CHUNK, (c + 1) * CHUNK)
            vn_ref[rows, :] = _ln(vg_ref[rows, :], gmg_ref[...], gmb_ref[...]).astype(bf16)
        return run

    def gating_task(g):
        def run():
            cols = slice(g * GM_GROUP_DIM, (g + 1) * GM_GROUP_DIM)
            rhs = jnp.concatenate([vn_ref[c * CHUNK:(c + 1) * CHUNK, cols] for c in range(nch)], axis=1)
            mixed = _dot(ws_ref[g], rhs)
            for c in range(nch):
                rows = slice(c * CHUNK, (c + 1) * CHUNK)
                ygm_ref[rows, cols] = (ug_ref[rows, cols].astype(f32)
                                       * (mixed[:, c * GM_GROUP_DIM:(c + 1) * GM_GROUP_DIM]
                                          + bsx_ref[:, cols])).astype(bf16)
        return run

    blocks = range(0, D, NBLK)
    gelu = lambda r, cols: jax.nn.gelu(r)
    u_t = [proj_task(wuvg_ref, 0, c0, ug_ref, gelu) for c0 in blocks]
    v_t = [proj_task(wuvg_ref, D, c0, vg_ref, gelu) for c0 in blocks]
    z_t = [proj_task(wzx_ref, 0, c0, sz_ref, lambda r, cols: _silu(r)) for c0 in blocks]
    gs_t = [proj_task(wuvg_ref, 2 * D, c0, gate_ref, lambda r, cols: _sigmoid(r + bg_ref[:, cols]))
            for c0 in blocks]
    gg_t = [proj_task(wuvg_ref, 3 * D, c0, gg_ref,
                      lambda r, cols: _sigmoid(r + bg_ref[:, D + cols.start:D + cols.stop])) for c0 in blocks]

    def gm_task(c0):
        def run():
            cols = slice(c0, c0 + NBLK)
            mg_ref[:, cols] = (gg_ref[:, cols].astype(f32) * _dot(ygm_ref[...], wgp_ref[:, cols])).astype(bf16)
        return run

    gm_t = [gm_task(c0) for c0 in blocks]
    e_t = [ext_task(c0) for c0 in range(0, XBC, XBC // 3)]
    c_t = [conv_task(c0) for c0 in range(0, XBC, LANES)]
    n_t = [vnorm_task(c) for c in range(nch)]
    s_t = [gating_task(g) for g in range(GM_GROUPS)]
    g_t = gg_t + [t_ for pair in zip(gs_t, gm_t) for t_ in pair]
    fill = (len(g_t) - len(z_t)) // nch
    assert (len(v_t), len(c_t), len(s_t)) == (4, 12, 8) and fill * nch == len(g_t) - len(z_t)

    for task in (e_t[0], v_t[0], c_t[0], e_t[1], c_t[1], v_t[1], c_t[2], e_t[2], c_t[3], v_t[2], c_t[4], v_t[3],
                 c_t[5]):
        task()
    dt = _softplus(_dot_nt(wdtT_ref[...], xm) + dtb_ref[...])
    dt_ref[...] = dt
    a_col = -jnp.exp(alog_ref[...])
    preps = [_ssd_prep(dt[HEADS:, c * CHUNK:(c + 1) * CHUNK], a_col[HEADS:], False) for c in range(nch)]
    for task in (u_t[0], c_t[6], u_t[1], c_t[7], *n_t[:nch // 2], u_t[2], c_t[8], *n_t[nch // 2:], u_t[3], c_t[9],
                 z_t[0], c_t[10], g_t[0], c_t[11], s_t[0], s_t[1]):
        task()
    for i in range(1, 4):
        for task in (z_t[i], g_t[i], s_t[2 * i], s_t[2 * i + 1]):
            task()
    rest = g_t[len(z_t):]
    for i, c in enumerate(range(nch - 1, -1, -1)):
        rows = slice(c * CHUNK, (c + 1) * CHUNK)
        y = _ssd_chunk(preps[c], xs_ref[rows, :], xsf_ref[rows, :], bc_ref[rows, 0:GROUPS * STATE],
                       bc_ref[rows, GROUPS * STATE:], hb_ref, fwd=False, fillers=rest[i * fill:(i + 1) * fill])
        yb_ref[rows, :] = y.astype(bf16)


def _pass1_call(x, mod3, ln0g, ln0b, w_zx, w_uvg, w_dtT, convw, convb, dtb_col, alog_col, s_b, gmg, gmb, ws, bsx,
                bg, wgp, t):
    bsz, seq, _ = x.shape
    nt = seq // t
    hb = t // HALO
    nhb = seq // HALO
    const = lambda shape: pl.BlockSpec(shape, lambda b, s: (0,) * len(shape), pipeline_mode=pl.Buffered(1))
    tok = lambda w: pl.BlockSpec((None, t, w), lambda b, s: (b, nt - 1 - s, 0))
    act = lambda w, dt_: jax.ShapeDtypeStruct((bsz, seq, w), dt_)
    return pl.pallas_call(
        _pass1_kernel,
        grid=(bsz, nt),
        in_specs=[tok(D),
                  pl.BlockSpec((None, HALO, D), lambda b, s: (b, jnp.maximum((nt - 1 - s) * hb - 1, 0), 0)),
                  pl.BlockSpec((None, HALO, D), lambda b, s: (b, jnp.minimum((nt - s) * hb, nhb - 1), 0)),
                  pl.BlockSpec((None, 1, 2 * D), lambda b, s: (b, 0, 0)),
                  const((1, D)), const((1, D)), const((D, D + XBC)), const((D, 4 * D)), const((2 * HEADS, D)),
                  const((CONV, XBC)), const((1, XBC)), const((2 * HEADS, 1)), const((2 * HEADS, 1)),
                  pl.BlockSpec((None, GROUPS, STATE, GROUP_W), lambda b, s: (b, 0, 0, 0)),
                  const((1, D)), const((1, D)), const((GM_GROUPS, CHUNK, CHUNK)), const((CHUNK, D)),
                  const((1, 2 * D)), const((D, D))],
        out_specs=[tok(D), tok(sum(PACKED)),
                   pl.BlockSpec((None, 2 * HEADS, t), lambda b, s: (b, 0, nt - 1 - s))],
        out_shape=[act(D, f32), act(sum(PACKED), bf16), jax.ShapeDtypeStruct((bsz, 2 * HEADS, seq), f32)],
        scratch_shapes=[pltpu.VMEM((t + 2 * HALO, XBC), f32), pltpu.VMEM((GROUPS, STATE, GROUP_W), f32),
                        pltpu.VMEM((t, D), f32), pltpu.VMEM((t, D), bf16), pltpu.VMEM((t, D), f32),
                        pltpu.VMEM((t, D), bf16), pltpu.VMEM((t, D), bf16), pltpu.VMEM((t, D), bf16)],
        compiler_params=pltpu.CompilerParams(dimension_semantics=("arbitrary", "arbitrary"),
                                             vmem_limit_bytes=VMEM_LIMIT),
        name="pass1_bwd",
    )(x, x, x, mod3, ln0g, ln0b, w_zx, w_uvg, w_dtT, convw, convb, dtb_col, alog_col, s_b, gmg, gmb, ws, bsx, bg,
      wgp)


def _pass2_kernel(nt, x0a_ref, moda_ref, modb_ref, pk_ref, dt_ref,
                  alog_ref, dsk_ref, ng_ref, wsp_ref, wo_ref, ln1g_ref, ln1b_ref, sf_ref,
                  w1_ref, w3_ref, w2_ref, ln2g_ref, ln2b_ref,
                  o_ref, hf_ref, ys_ref, x1_ref, hm_ref, a_ref, ff_ref):
    sz_ref, xs_ref, bc_ref, yb_ref, mg_ref, gate_ref = _unpack(pk_ref)
    t = x0a_ref.shape[0]
    nch = t // CHUNK
    step = pl.program_id(0)

    @pl.when(step == 0)
    def _():
        x1_ref[...] = jnp.zeros(x1_ref.shape, f32)

    @pl.when(step % nt == 0)
    def _():
        hf_ref[...] = sf_ref[...]

    hm_ref[...] = (x1_ref[...] * (1.0 + modb_ref[:, 4 * D:5 * D]) + modb_ref[:, 3 * D:4 * D]).astype(bf16)

    def up_task(n0):
        def run():
            cols = slice(n0, n0 + NBLK)
            a_ref[:, cols] = (_silu(_dot(hm_ref[...], w1_ref[:, cols])) * _dot(hm_ref[...], w3_ref[:, cols])).astype(bf16)
        return run

    def down_task(c0):
        def run():
            ff_ref[:, c0:c0 + NBLK] = _dot(a_ref[...], w2_ref[:, c0:c0 + NBLK])
        return run

    tasks = [up_task(n0) for n0 in range(0, D_FF, NBLK)] + [down_task(c0) for c0 in range(0, D, NBLK)]
    per = -(-len(tasks) // nch)

    a_col = -jnp.exp(alog_ref[...])
    dt = dt_ref[...]
    preps = [_ssd_prep(dt[0:HEADS, c * CHUNK:(c + 1) * CHUNK], a_col[0:HEADS], True) for c in range(nch)]
    dsk = dsk_ref[0:1, :] + dsk_ref[1:2, :]
    for c in range(nch):
        rows = slice(c * CHUNK, (c + 1) * CHUNK)
        xs_bf = xs_ref[rows, :]
        xs_f = xs_bf.astype(f32)
        y = _ssd_chunk(preps[c], xs_bf, xs_f, bc_ref[rows, 0:GROUPS * STATE], bc_ref[rows, GROUPS * STATE:],
                       hf_ref, fwd=True, fillers=tasks[c * per:(c + 1) * per])
        y = y + yb_ref[rows, :].astype(f32) + xs_f * dsk
        hh = y * sz_ref[rows, :].astype(f32)
        hh = hh * lax.rsqrt(jnp.mean(hh * hh, axis=-1, keepdims=True) + LN_EPS) * ng_ref[...]
        ys_ref[rows, :] = hh.astype(bf16)

    o_ref[...] = _ln(ALPHA * x1_ref[...] + modb_ref[:, 5 * D:6 * D] * ff_ref[...], ln2g_ref[...], ln2b_ref[...])

    merged = gate_ref[...].astype(f32) * _dot(ys_ref[...], wsp_ref[...]) + mg_ref[...].astype(f32)
    out_x = _dot(merged.astype(bf16), wo_ref[...])
    x1_ref[...] = _ln(x0a_ref[...] + moda_ref[:, 2 * D:3 * D] * out_x, ln1g_ref[...], ln1b_ref[...])


def _pass2_call(x0a, mod3, packed, dt, alog_col, dsk, ng, wsp, wo, ln1g, ln1b, s_f, w1, w3, w2, ln2g, ln2b, t):
    bsz, seq, _ = x0a.shape
    nt = seq // t
    last = bsz * nt - 1
    cur = lambda s: jnp.minimum(s, last)
    prev = lambda s: jnp.maximum(s - 1, 0)
    const = lambda shape: pl.BlockSpec(shape, lambda s: (0,) * len(shape), pipeline_mode=pl.Buffered(1))
    tok = lambda w: pl.BlockSpec((None, t, w), lambda s: (cur(s) // nt, cur(s) % nt, 0))
    return pl.pallas_call(
        functools.partial(_pass2_kernel, nt),
        grid=(bsz * nt + 1,),
        in_specs=[tok(D), pl.BlockSpec((None, 1, D_MOD), lambda s: (cur(s) // nt, 0, 0)),
                  pl.BlockSpec((None, 1, D_MOD), lambda s: (prev(s) // nt, 0, 0)),
                  tok(sum(PACKED)),
                  pl.BlockSpec((None, 2 * HEADS, t), lambda s: (cur(s) // nt, 0, cur(s) % nt)),
                  const((2 * HEADS, 1)), const((2, D)), const((1, D)),
                  const((D, D)), const((D, D)), const((1, D)), const((1, D)),
                  pl.BlockSpec((None, GROUPS, STATE, GROUP_W), lambda s: (cur(s) // nt, 0, 0, 0)),
                  const((D, D_FF)), const((D, D_FF)), const((D_FF, D)), const((1, D)), const((1, D))],
        out_specs=pl.BlockSpec((None, t, D), lambda s: (prev(s) // nt, prev(s) % nt, 0)),
        out_shape=jax.ShapeDtypeStruct((bsz, seq, D), f32),
        scratch_shapes=[pltpu.VMEM((GROUPS, STATE, GROUP_W), f32), pltpu.VMEM((t, D), bf16),
                        pltpu.VMEM((t, D), f32), pltpu.VMEM((t, D), bf16),
                        pltpu.VMEM((t, D_FF), bf16), pltpu.VMEM((t, D), f32)],
        compiler_params=pltpu.CompilerParams(dimension_semantics=("arbitrary",), vmem_limit_bytes=VMEM_LIMIT),
        name="pass2_ffn",
    )(x0a, mod3, mod3, packed, dt, alog_col, dsk, ng, wsp, wo, ln1g, ln1b, s_f, w1, w3, w2, ln2g, ln2b)


def kernel(x, c, ctx, c_ctx, ln0_g, ln0_b, w_ada, b_ada, w_in, conv_w, conv_b, dt_bias, a_log, d_skip, ssd_norm_g, gm_norm_g, gm_norm_b, w_spatial, b_spatial, b_gate, w_ssd_proj, w_gm_proj, w_out, ln1_g, ln1_b, w_ff1, w_ff3, w_ff2, ln2_g, ln2_b):
    bsz, seq, _ = x.shape
    assert x.shape[2] == D and w_in.shape == (DEPTH, D, D_PROJ) and bsz < 16
    assert seq % 512 == 0 and ctx.shape[1] % CHUNK == 0
    row = lambda a: a.reshape(1, -1)

    w_in0 = w_in[0]
    o_dt = D + XBC
    w_zx = w_in0[:, :o_dt].astype(bf16)
    w_uvg = w_in0[:, o_dt + 2 * HEADS:].astype(bf16)
    w_dtT = w_in0[:, o_dt:o_dt + 2 * HEADS].T.astype(bf16)
    dtb_col = dt_bias[0].reshape(2 * HEADS, 1)
    alog_col = a_log[0].reshape(2 * HEADS, 1)
    dsk = jnp.repeat(d_skip[0], HEAD_DIM, axis=1)
    bsx = jnp.repeat(b_spatial[0].T, GM_GROUP_DIM, axis=1)
    ln0g, ln0b = row(ln0_g), row(ln0_b)

    cc = jnp.concatenate([c, c_ctx[None, :], jnp.zeros((15 - bsz, D), f32)], axis=0)
    mod = _mod_call(cc, w_ada[0], row(b_ada[0]))
    mod3 = mod.reshape(16, 1, D_MOD)
    s_f, s_b = _ctx_call(ctx, mod3[bsz:bsz + 1], ln0g, ln0b, w_zx, w_dtT,
                         conv_w[0], row(conv_b[0]), dtb_col, alog_col)

    t = 256
    x0a, packed, dt = _pass1_call(
        x, mod3, ln0g, ln0b, w_zx, w_uvg, w_dtT, conv_w[0], row(conv_b[0]), dtb_col, alog_col, s_b,
        row(gm_norm_g[0]), row(gm_norm_b[0]), w_spatial[0].astype(bf16), bsx, row(b_gate[0]),
        w_gm_proj[0].astype(bf16), 2 * t)
    return _pass2_call(x0a, mod3, packed, dt, alog_col, dsk, row(ssd_norm_g[0]),
                       w_ssd_proj[0].astype(bf16), w_out[0].astype(bf16),
                       row(ln1_g[0]), row(ln1_b[0]), s_f,
                       w_ff1[0].astype(bf16), w_ff3[0].astype(bf16), w_ff2[0].astype(bf16),
                       row(ln2_g[0]), row(ln2_b[0]), 2 * t)
```

```python
import functools

import jax
import jax.numpy as jnp
from jax import lax
from jax.experimental import pallas as pl
from jax.experimental.pallas import tpu as pltpu

f32 = jnp.float32
bf16 = jnp.bfloat16

D = 1024
HEADS = 16
HEAD_DIM = 64
GROUPS = 2
GROUP_W = D // GROUPS
STATE = 128
CONV = 5
CHUNK = 128
XBC = D + 2 * GROUPS * STATE
GM_GROUPS = 8
GM_GROUP_DIM = D // GM_GROUPS
D_FF = 2816
D_MOD = 6 * D
DEPTH = 1
ALPHA = (2 * DEPTH) ** 0.25
LN_EPS = 1e-5
HALO = 8
LANES = 128
NBLK = 256

D_PROJ = D + XBC + 2 * HEADS + 4 * D

VMEM_LIMIT = 56 * 1024 * 1024


def _dot(a, b):
    return jnp.dot(a, b, preferred_element_type=f32)


def _dot_nt(a, b):
    return lax.dot_general(a, b, (((1,), (1,)), ((), ())), preferred_element_type=f32)


def _dot_tn(a, b):
    return lax.dot_general(a, b, (((0,), (0,)), ((), ())), preferred_element_type=f32)


def _norm_rows(xf):
    mu = jnp.mean(xf, axis=-1, keepdims=True)
    xc = xf - mu
    return xc * lax.rsqrt(jnp.mean(xc * xc, axis=-1, keepdims=True) + LN_EPS)


def _ln(xf, g, b):
    return _norm_rows(xf) * g + b


def _sigmoid(x):
    return 0.5 * jnp.tanh(0.5 * x) + 0.5


def _silu(x):
    h = 0.5 * x
    return h * jnp.tanh(h) + h


def _softplus(x):
    return jnp.maximum(x, 0.0) + jnp.log1p(jnp.exp(-jnp.abs(x)))


def _scan_rows(a, tri):
    hi = a.astype(bf16)
    r1 = a - hi.astype(f32)
    mid = r1.astype(bf16)
    lo = (r1 - mid.astype(f32)).astype(bf16)
    cs = _dot(jnp.concatenate([hi, mid, lo], axis=0), tri)
    return cs[0:16] + cs[16:32] + cs[32:48]


def _dir_rows(dt, a_col, fwd):
    k = lax.broadcasted_iota(jnp.int32, (CHUNK, CHUNK), 0)
    j = lax.broadcasted_iota(jnp.int32, (CHUNK, CHUNK), 1)
    tri = jnp.where((k <= j) if fwd else (k >= j), 1.0, 0.0).astype(bf16)
    cum = _scan_rows(dt * a_col, tri)
    total = cum[:, CHUNK - 1:CHUNK] if fwd else cum[:, 0:1]
    return cum, jnp.exp(total - cum) * dt, jnp.exp(cum)


def _to_cols(rows):
    pad = jnp.zeros((LANES - 16 * len(rows), CHUNK), f32)
    return jnp.concatenate(list(rows) + [pad], axis=0).T


def _expand_heads(m):
    lane = lax.broadcasted_iota(jnp.int32, (CHUNK, LANES), 1)
    pieces = []
    for p in range(HEADS // 2):
        pieces.append(jnp.where(lane < HEAD_DIM, m[:, 2 * p:2 * p + 1], m[:, 2 * p + 1:2 * p + 2]))
    return jnp.concatenate(pieces, axis=1)


def _ssd_prep(dt, a_col, fwd):
    cum, wdt, ecum = _dir_rows(dt, a_col, fwd)
    return cum - jnp.log(dt), _to_cols([cum, wdt, ecum])


def _ssd_chunk(prep, xs_bf, xs_f, bm, cm, h_ref, fwd, want_y=True, fillers=()):
    cumj, cols = prep
    fillers = list(fillers)
    ecum_x = _expand_heads(cols[:, 32:48])
    y = None
    if want_y:
        ii = lax.broadcasted_iota(jnp.int32, (CHUNK, CHUNK), 0)
        jj = lax.broadcasted_iota(jnp.int32, (CHUNK, CHUNK), 1)
        keep = (ii >= jj) if fwd else (ii <= jj)
        lane = lax.broadcasted_iota(jnp.int32, (CHUNK, LANES), 1)
        cb = [_dot_nt(cm[:, g * STATE:(g + 1) * STATE], bm[:, g * STATE:(g + 1) * STATE]) for g in range(GROUPS)]
        pieces = []
        for p in range(HEADS // 2):
            g = (2 * p) // (HEADS // GROUPS)
            ws = []
            for h in (2 * p, 2 * p + 1):
                seg = cols[:, h:h + 1] - cumj[h:h + 1, :]
                ws.append(jnp.where(keep, cb[g] * jnp.exp(seg), 0.0))
            lhs = jnp.concatenate(ws, axis=1).astype(bf16)
            x2 = xs_bf[:, p * LANES:(p + 1) * LANES]
            zero = jnp.zeros_like(x2)
            rhs = jnp.concatenate([jnp.where(lane < HEAD_DIM, x2, zero),
                                   jnp.where(lane >= HEAD_DIM, x2, zero)], axis=0)
            pieces.append(_dot(lhs, rhs))
            if fillers:
                fillers.pop(0)()
        y_diag = jnp.concatenate(pieces, axis=1)
        y_off = jnp.concatenate([_dot(cm[:, g * STATE:(g + 1) * STATE], h_ref[g].astype(bf16))
                                 for g in range(GROUPS)], axis=1)
        y = y_diag + y_off * ecum_x
    xw = (xs_f * _expand_heads(cols[:, 16:32])).astype(bf16)
    dec = ecum_x[CHUNK - 1:CHUNK, :] if fwd else ecum_x[0:1, :]
    for g in range(GROUPS):
        st = _dot_tn(bm[:, g * STATE:(g + 1) * STATE], xw[:, g * GROUP_W:(g + 1) * GROUP_W])
        h_ref[g] = dec[:, g * GROUP_W:(g + 1) * GROUP_W] * h_ref[g] + st
    for f in fillers:
        f()
    return y


def _conv_silu(ext_ref, convw_ref, convb_ref, t, c0, c1):
    n = (t + 2 * HALO) // 8
    w = c1 - c0
    p = ext_ref[:, c0:c1].reshape(n, 8, w)
    sub = lax.broadcasted_iota(jnp.int32, (1, 8, w), 1)
    mid = CONV // 2
    acc = convb_ref[:, c0:c1].reshape(1, 1, w) + p[1:n - 1] * convw_ref[mid:mid + 1, c0:c1].reshape(1, 1, w)
    for k in range(CONV):
        s = k - mid
        if s > 0:
            mixed = jnp.where(sub >= s, p[1:n - 1], p[2:n])
            shifted = pltpu.roll(mixed, 8 - s, axis=1)
        elif s < 0:
            mixed = jnp.where(sub < 8 + s, p[1:n - 1], p[0:n - 2])
            shifted = pltpu.roll(mixed, -s, axis=1)
        else:
            continue
        acc = acc + shifted * convw_ref[k:k + 1, c0:c1].reshape(1, 1, w)
    return _silu(acc).reshape(t, w)


def _mod_kernel(c_ref, w_ref, b_ref, o_ref):
    s = _silu(c_ref[...]).astype(bf16)
    o_ref[...] = _dot(s, w_ref[...].astype(bf16)) + b_ref[...]


def _mod_call(cc, w_ada, b_ada):
    nb = 1024
    return pl.pallas_call(
        _mod_kernel,
        grid=(D_MOD // nb,),
        in_specs=[pl.BlockSpec((16, D), lambda n: (0, 0)),
                  pl.BlockSpec((D, nb), lambda n: (0, n)),
                  pl.BlockSpec((1, nb), lambda n: (0, n))],
        out_specs=pl.BlockSpec((16, nb), lambda n: (0, n)),
        out_shape=jax.ShapeDtypeStruct((16, D_MOD), f32),
        compiler_params=pltpu.CompilerParams(dimension_semantics=("arbitrary",), vmem_limit_bytes=VMEM_LIMIT),
        name="adaln_mod",
    )(cc, w_ada, b_ada)


def _ctx_kernel(ctx_ref, mod_ref, ln0g_ref, ln0b_ref, wzx_ref, wdtT_ref, convw_ref, convb_ref, dtb_ref,
                alog_ref, sf_ref, sb_ref, ext_ref, xs_ref, bc_ref):
    cl = ctx_ref.shape[0]
    sh1 = mod_ref[:, 0:D]
    sc1 = mod_ref[:, D:2 * D]
    xm = (_ln(ctx_ref[...], ln0g_ref[...], ln0b_ref[...]) * (1.0 + sc1) + sh1).astype(bf16)
    ext_ref[pl.ds(0, HALO), :] = jnp.zeros((HALO, XBC), f32)
    ext_ref[pl.ds(HALO + cl, HALO), :] = jnp.zeros((HALO, XBC), f32)
    ext_ref[pl.ds(HALO, cl), :] = _dot(xm, wzx_ref[:, D:])
    for c0 in range(0, D, NBLK):
        xs_ref[:, c0:c0 + NBLK] = _conv_silu(ext_ref, convw_ref, convb_ref, cl, c0, c0 + NBLK)
    bc_ref[...] = _conv_silu(ext_ref, convw_ref, convb_ref, cl, D, XBC).astype(bf16)
    dt = _softplus(_dot_nt(wdtT_ref[...], xm) + dtb_ref[...])
    a_col = -jnp.exp(alog_ref[...])
    sf_ref[...] = jnp.zeros(sf_ref.shape, f32)
    sb_ref[...] = jnp.zeros(sb_ref.shape, f32)
    nch = cl // CHUNK
    jobs = [(fwd, h_ref, c, 0 if fwd else HEADS)
            for k in range(nch) for fwd, h_ref, c in ((True, sf_ref, k), (False, sb_ref, nch - 1 - k))]
    preps = [_ssd_prep(dt[r0:r0 + HEADS, c * CHUNK:(c + 1) * CHUNK], a_col[r0:r0 + HEADS], fwd)
             for fwd, _, c, r0 in jobs]
    for prep, (fwd, h_ref, c, _) in zip(preps, jobs):
        rows = slice(c * CHUNK, (c + 1) * CHUNK)
        _ssd_chunk(prep, None, xs_ref[rows, :], bc_ref[rows, 0:GROUPS * STATE], None, h_ref, fwd, want_y=False)


def _ctx_call(ctx, mod3, ln0g, ln0b, w_zx, w_dtT, convw, convb, dtb_col, alog_col):
    bsz, cl, _ = ctx.shape
    full = lambda shape: pl.BlockSpec(shape, lambda b: (0,) * len(shape))
    st_shape = jax.ShapeDtypeStruct((bsz, GROUPS, STATE, GROUP_W), f32)
    st_spec = pl.BlockSpec((None, GROUPS, STATE, GROUP_W), lambda b: (b, 0, 0, 0))
    return pl.pallas_call(
        _ctx_kernel,
        grid=(bsz,),
        in_specs=[pl.BlockSpec((None, cl, D), lambda b: (b, 0, 0)),
                  pl.BlockSpec((None, 1, 2 * D), lambda b: (0, 0, 0)),
                  full((1, D)), full((1, D)), full((D, D + XBC)), full((2 * HEADS, D)),
                  full((CONV, XBC)), full((1, XBC)), full((2 * HEADS, 1)), full((2 * HEADS, 1))],
        out_specs=[st_spec, st_spec],
        out_shape=[st_shape, st_shape],
        scratch_shapes=[pltpu.VMEM((cl + 2 * HALO, XBC), f32), pltpu.VMEM((cl, D), f32),
                        pltpu.VMEM((cl, 2 * GROUPS * STATE), bf16)],
        compiler_params=pltpu.CompilerParams(dimension_semantics=("arbitrary",), vmem_limit_bytes=VMEM_LIMIT),
        name="ctx_states",
    )(ctx, mod3, ln0g, ln0b, w_zx, w_dtT, convw, convb, dtb_col, alog_col)


def _pass1_kernel(x_ref, xp_ref, xn_ref, mod_ref, ln0g_ref, ln0b_ref, wzx_ref, wuvg_ref, wdtT_ref, convw_ref,
                  convb_ref, dtb_ref, alog_ref, sb_ref, gmg_ref, gmb_ref, ws_ref, bsx_ref, bg_ref, wgp_ref,
                  x0a_ref, sz_ref, xs_ref, bc_ref, yb_ref, mg_ref, gate_ref, dt_ref,
                  ext_ref, hb_ref, xsf_ref, ug_ref, vg_ref, vn_ref, ygm_ref, gg_ref):
    t = x_ref.shape[0]
    nch = t // CHUNK
    step = pl.program_id(1)
    nt = pl.num_programs(1)
    tile = nt - 1 - step

    @pl.when(step == 0)
    def _():
        hb_ref[...] = sb_ref[...]

    g0 = ln0g_ref[...]
    b0 = ln0b_ref[...]
    sc1p = 1.0 + mod_ref[:, D:2 * D]
    gm = g0 * sc1p
    bm = b0 * sc1p + mod_ref[:, 0:D]

    xn = _norm_rows(x_ref[...])
    x0a_ref[...] = xn * (ALPHA * g0) + ALPHA * b0
    xm_f = xn * gm + bm
    xm = xm_f.astype(bf16)
    ext = jnp.concatenate([_norm_rows(xp_ref[...]) * gm + bm, xm_f, _norm_rows(xn_ref[...]) * gm + bm],
                          axis=0).astype(bf16)
    row = lax.broadcasted_iota(jnp.int32, (t + 2 * HALO, 1), 0)
    valid = jnp.logical_and(jnp.logical_or(row >= HALO, tile > 0), jnp.logical_or(row < HALO + t, tile < nt - 1))

    def ext_task(c0):
        def run():
            w = XBC // 3
            ext_ref[:, c0:c0 + w] = jnp.where(valid, _dot(ext, wzx_ref[:, D + c0:D + c0 + w]), 0.0)
        return run

    def proj_task(w_ref, w0, c0, dst_ref, fn):
        def run():
            cols = slice(c0, c0 + NBLK)
            dst_ref[:, cols] = fn(_dot(xm, w_ref[:, w0 + c0:w0 + c0 + NBLK]), cols).astype(dst_ref.dtype)
        return run

    def conv_task(c0):
        def run():
            blk = _conv_silu(ext_ref, convw_ref, convb_ref, t, c0, c0 + LANES)
            if c0 < D:
                xsf_ref[:, c0:c0 + LANES] = blk
                xs_ref[:, c0:c0 + LANES] = blk.astype(bf16)
            else:
                bc_ref[:, c0 - D:c0 - D + LANES] = blk.astype(bf16)
        return run

    def vnorm_task(c):
        def run():
            rows = slice(c * CHUNK, (c + 1) * CHUNK)
            vn_ref[rows, :] = _ln(vg_ref[rows, :], gmg_ref[...], gmb_ref[...]).astype(bf16)
        return run

    def gating_task(g):
        def run():
            cols = slice(g * GM_GROUP_DIM, (g + 1) * GM_GROUP_DIM)
            rhs = jnp.concatenate([vn_ref[c * CHUNK:(c + 1) * CHUNK, cols] for c in range(nch)], axis=1)
            mixed = _dot(ws_ref[g], rhs)
            for c in range(nch):
                rows = slice(c * CHUNK, (c + 1) * CHUNK)
                ygm_ref[rows, cols] = (ug_ref[rows, cols].astype(f32)
                                       * (mixed[:, c * GM_GROUP_DIM:(c + 1) * GM_GROUP_DIM]
                                          + bsx_ref[:, cols])).astype(bf16)
        return run

    blocks = range(0, D, NBLK)
    gelu = lambda r, cols: jax.nn.gelu(r)
    u_t = [proj_task(wuvg_ref, 0, c0, ug_ref, gelu) for c0 in blocks]
    v_t = [proj_task(wuvg_ref, D, c0, vg_ref, gelu) for c0 in blocks]
    z_t = [proj_task(wzx_ref, 0, c0, sz_ref, lambda r, cols: _silu(r.astype(bf16))) for c0 in blocks]
    gs_t = [proj_task(wuvg_ref, 2 * D, c0, gate_ref, lambda r, cols: _sigmoid((r + bg_ref[:, cols]).astype(bf16)))
            for c0 in blocks]
    gg_t = [proj_task(wuvg_ref, 3 * D, c0, gg_ref,
                      lambda r, cols: _sigmoid(r + bg_ref[:, D + cols.start:D + cols.stop])) for c0 in blocks]

    def gm_task(c0):
        def run():
            cols = slice(c0, c0 + NBLK)
            mg_ref[:, cols] = (gg_ref[:, cols] * _dot(ygm_ref[...], wgp_ref[:, cols])).astype(bf16)
        return run

    gm_t = [gm_task(c0) for c0 in blocks]
    e_t = [ext_task(c0) for c0 in range(0, XBC, XBC // 3)]
    c_t = [conv_task(c0) for c0 in range(0, XBC, LANES)]
    n_t = [vnorm_task(c) for c in range(nch)]
    s_t = [gating_task(g) for g in range(GM_GROUPS)]
    g_t = gg_t + [t_ for pair in zip(gs_t, gm_t) for t_ in pair]
    fill = (len(g_t) - len(z_t)) // nch
    assert (len(v_t), len(c_t), len(s_t)) == (4, 12, 8) and fill * nch == len(g_t) - len(z_t)

    for task in (v_t[0], e_t[0], v_t[1], e_t[1], v_t[2], e_t[2], v_t[3]):
        task()
    dt = _softplus(_dot_nt(wdtT_ref[...], xm) + dtb_ref[...])
    dt_ref[...] = dt
    a_col = -jnp.exp(alog_ref[...])
    preps = [_ssd_prep(dt[HEADS:, c * CHUNK:(c + 1) * CHUNK], a_col[HEADS:], False) for c in range(nch)]
    assert nch == 2
    for task in (u_t[0], c_t[0], u_t[1], c_t[1], n_t[0], u_t[2], c_t[2], n_t[1], u_t[3], c_t[3]):
        task()
    for i in range(4):
        for task in (z_t[i], c_t[4 + 2 * i], g_t[i], c_t[5 + 2 * i], s_t[2 * i], s_t[2 * i + 1]):
            task()
    rest = g_t[len(z_t):]
    for i, c in enumerate(range(nch - 1, -1, -1)):
        rows = slice(c * CHUNK, (c + 1) * CHUNK)
        y = _ssd_chunk(preps[c], xs_ref[rows, :], xsf_ref[rows, :], bc_ref[rows, 0:GROUPS * STATE],
                       bc_ref[rows, GROUPS * STATE:], hb_ref, fwd=False, fillers=rest[i * fill:(i + 1) * fill])
        yb_ref[rows, :] = y.astype(bf16)


def _pass1_call(x, mod3, ln0g, ln0b, w_zx, w_uvg, w_dtT, convw, convb, dtb_col, alog_col, s_b, gmg, gmb, ws, bsx,
                bg, wgp, t):
    bsz, seq, _ = x.shape
    nt = seq // t
    hb = t // HALO
    nhb = seq // HALO
    const = lambda shape: pl.BlockSpec(shape, lambda b, s: (0,) * len(shape), pipeline_mode=pl.Buffered(1))
    tok = lambda w: pl.BlockSpec((None, t, w), lambda b, s: (b, nt - 1 - s, 0))
    act = lambda w, dt_: jax.ShapeDtypeStruct((bsz, seq, w), dt_)
    return pl.pallas_call(
        _pass1_kernel,
        grid=(bsz, nt),
        in_specs=[tok(D),
                  pl.BlockSpec((None, HALO, D), lambda b, s: (b, jnp.maximum((nt - 1 - s) * hb - 1, 0), 0)),
                  pl.BlockSpec((None, HALO, D), lambda b, s: (b, jnp.minimum((nt - s) * hb, nhb - 1), 0)),
                  pl.BlockSpec((None, 1, 2 * D), lambda b, s: (b, 0, 0)),
                  const((1, D)), const((1, D)), const((D, D + XBC)), const((D, 4 * D)), const((2 * HEADS, D)),
                  const((CONV, XBC)), const((1, XBC)), const((2 * HEADS, 1)), const((2 * HEADS, 1)),
                  pl.BlockSpec((None, GROUPS, STATE, GROUP_W), lambda b, s: (b, 0, 0, 0)),
                  const((1, D)), const((1, D)), const((GM_GROUPS, CHUNK, CHUNK)), const((CHUNK, D)),
                  const((1, 2 * D)), const((D, D))],
        out_specs=[tok(D), tok(D), tok(D), tok(2 * GROUPS * STATE), tok(D), tok(D), tok(D),
                   pl.BlockSpec((None, 2 * HEADS, t), lambda b, s: (b, 0, nt - 1 - s))],
        out_shape=[act(D, f32), act(D, bf16), act(D, bf16), act(2 * GROUPS * STATE, bf16), act(D, bf16),
                   act(D, bf16), act(D, bf16), jax.ShapeDtypeStruct((bsz, 2 * HEADS, seq), f32)],
        scratch_shapes=[pltpu.VMEM((t + 2 * HALO, XBC), f32), pltpu.VMEM((GROUPS, STATE, GROUP_W), f32),
                        pltpu.VMEM((t, D), f32), pltpu.VMEM((t, D), bf16), pltpu.VMEM((t, D), f32),
                        pltpu.VMEM((t, D), bf16), pltpu.VMEM((t, D), bf16), pltpu.VMEM((t, D), f32)],
        compiler_params=pltpu.CompilerParams(dimension_semantics=("arbitrary", "arbitrary"),
                                             vmem_limit_bytes=VMEM_LIMIT),
        name="pass1_bwd",
    )(x, x, x, mod3, ln0g, ln0b, w_zx, w_uvg, w_dtT, convw, convb, dtb_col, alog_col, s_b, gmg, gmb, ws, bsx, bg,
      wgp)


def _pass2_kernel(nt, x0a_ref, moda_ref, modb_ref, sz_ref, xs_ref, bc_ref, yb_ref, mg_ref, gate_ref, dt_ref,
                  alog_ref, dsk_ref, ng_ref, wsp_ref, wo_ref, ln1g_ref, ln1b_ref, sf_ref,
                  w1_ref, w3_ref, w2_ref, ln2g_ref, ln2b_ref,
                  o_ref, hf_ref, ys_ref, x1_ref, hm_ref, a_ref, ff_ref):
    t = x0a_ref.shape[0]
    nch = t // CHUNK
    step = pl.program_id(0)

    @pl.when(step == 0)
    def _():
        x1_ref[...] = jnp.zeros(x1_ref.shape, f32)

    @pl.when(step % nt == 0)
    def _():
        hf_ref[...] = sf_ref[...]

    hm_ref[...] = (x1_ref[...] * (1.0 + modb_ref[:, 4 * D:5 * D]) + modb_ref[:, 3 * D:4 * D]).astype(bf16)

    def up_task(n0):
        def run():
            cols = slice(n0, n0 + NBLK)
            a_ref[:, cols] = (_silu(_dot(hm_ref[...], w1_ref[:, cols])) * _dot(hm_ref[...], w3_ref[:, cols])).astype(bf16)
        return run

    def down_task(c0):
        def run():
            ff_ref[:, c0:c0 + NBLK] = _dot(a_ref[...], w2_ref[:, c0:c0 + NBLK])
        return run

    tasks = [up_task(n0) for n0 in range(0, D_FF, NBLK)] + [down_task(c0) for c0 in range(0, D, NBLK)]
    per = -(-len(tasks) // nch)

    a_col = -jnp.exp(alog_ref[...])
    dt = dt_ref[...]
    preps = [_ssd_prep(dt[0:HEADS, c * CHUNK:(c + 1) * CHUNK], a_col[0:HEADS], True) for c in range(nch)]
    dsk = dsk_ref[0:1, :] + dsk_ref[1:2, :]
    for c in range(nch):
        rows = slice(c * CHUNK, (c + 1) * CHUNK)
        xs_bf = xs_ref[rows, :]
        xs_f = xs_bf.astype(f32)
        y = _ssd_chunk(preps[c], xs_bf, xs_f, bc_ref[rows, 0:GROUPS * STATE], bc_ref[rows, GROUPS * STATE:],
                       hf_ref, fwd=True, fillers=tasks[c * per:(c + 1) * per])
        y = y + yb_ref[rows, :].astype(f32) + xs_f * dsk
        hh = y * sz_ref[rows, :].astype(f32)
        hh = hh * lax.rsqrt(jnp.mean(hh * hh, axis=-1, keepdims=True) + LN_EPS) * ng_ref[...]
        ys_ref[rows, :] = hh.astype(bf16)

    o_ref[...] = _ln(ALPHA * x1_ref[...] + modb_ref[:, 5 * D:6 * D] * ff_ref[...], ln2g_ref[...], ln2b_ref[...])

    merged = gate_ref[...].astype(f32) * _dot(ys_ref[...], wsp_ref[...]) + mg_ref[...].astype(f32)
    out_x = _dot(merged.astype(bf16), wo_ref[...])
    x1_ref[...] = _ln(x0a_ref[...] + moda_ref[:, 2 * D:3 * D] * out_x, ln1g_ref[...], ln1b_ref[...])


def _pass2_call(x0a, mod3, sz, xs, bc, yb, mg, gate, dt, alog_col, dsk, ng, wsp, wo, ln1g, ln1b, s_f,
                w1, w3, w2, ln2g, ln2b, t):
    bsz, seq, _ = x0a.shape
    nt = seq // t
    last = bsz * nt - 1
    cur = lambda s: jnp.minimum(s, last)
    prev = lambda s: jnp.maximum(s - 1, 0)
    const = lambda shape: pl.BlockSpec(shape, lambda s: (0,) * len(shape), pipeline_mode=pl.Buffered(1))
    tok = lambda w: pl.BlockSpec((None, t, w), lambda s: (cur(s) // nt, cur(s) % nt, 0))
    return pl.pallas_call(
        functools.partial(_pass2_kernel, nt),
        grid=(bsz * nt + 1,),
        in_specs=[tok(D), pl.BlockSpec((None, 1, D_MOD), lambda s: (cur(s) // nt, 0, 0)),
                  pl.BlockSpec((None, 1, D_MOD), lambda s: (prev(s) // nt, 0, 0)),
                  tok(D), tok(D), tok(2 * GROUPS * STATE), tok(D), tok(D), tok(D),
                  pl.BlockSpec((None, 2 * HEADS, t), lambda s: (cur(s) // nt, 0, cur(s) % nt)),
                  const((2 * HEADS, 1)), const((2, D)), const((1, D)),
                  const((D, D)), const((D, D)), const((1, D)), const((1, D)),
                  pl.BlockSpec((None, GROUPS, STATE, GROUP_W), lambda s: (cur(s) // nt, 0, 0, 0)),
                  const((D, D_FF)), const((D, D_FF)), const((D_FF, D)), const((1, D)), const((1, D))],
        out_specs=pl.BlockSpec((None, t, D), lambda s: (prev(s) // nt, prev(s) % nt, 0)),
        out_shape=jax.ShapeDtypeStruct((bsz, seq, D), f32),
        scratch_shapes=[pltpu.VMEM((GROUPS, STATE, GROUP_W), f32), pltpu.VMEM((t, D), bf16),
                        pltpu.VMEM((t, D), f32), pltpu.VMEM((t, D), bf16),
                        pltpu.VMEM((t, D_FF), bf16), pltpu.VMEM((t, D), f32)],
        compiler_params=pltpu.CompilerParams(dimension_semantics=("arbitrary",), vmem_limit_bytes=VMEM_LIMIT),
        name="pass2_ffn",
    )(x0a, mod3, mod3, sz, xs, bc, yb, mg, gate, dt, alog_col, dsk, ng, wsp, wo, ln1g, ln1b, s_f,
      w1, w3, w2, ln2g, ln2b)


def kernel(x, c, ctx, c_ctx, ln0_g, ln0_b, w_ada, b_ada, w_in, conv_w, conv_b, dt_bias, a_log, d_skip, ssd_norm_g, gm_norm_g, gm_norm_b, w_spatial, b_spatial, b_gate, w_ssd_proj, w_gm_proj, w_out, ln1_g, ln1_b, w_ff1, w_ff3, w_ff2, ln2_g, ln2_b):
    bsz, seq, _ = x.shape
    assert x.shape[2] == D and w_in.shape == (DEPTH, D, D_PROJ) and bsz < 16
    assert seq % 512 == 0 and ctx.shape[1] % CHUNK == 0
    row = lambda a: a.reshape(1, -1)

    w_in0 = w_in[0]
    o_dt = D + XBC
    w_zx = w_in0[:, :o_dt].astype(bf16)
    w_uvg = w_in0[:, o_dt + 2 * HEADS:].astype(bf16)
    w_dtT = w_in0[:, o_dt:o_dt + 2 * HEADS].T.astype(bf16)
    dtb_col = dt_bias[0].reshape(2 * HEADS, 1)
    alog_col = a_log[0].reshape(2 * HEADS, 1)
    dsk = jnp.repeat(d_skip[0], HEAD_DIM, axis=1)
    bsx = jnp.repeat(b_spatial[0].T, GM_GROUP_DIM, axis=1)
    ln0g, ln0b = row(ln0_g), row(ln0_b)

    cc = jnp.concatenate([c, c_ctx[None, :], jnp.zeros((15 - bsz, D), f32)], axis=0)
    mod = _mod_call(cc, w_ada[0], row(b_ada[0]))
    mod3 = mod.reshape(16, 1, D_MOD)
    s_f, s_b = _ctx_call(ctx, mod3[bsz:bsz + 1], ln0g, ln0b, w_zx, w_dtT,
                         conv_w[0], row(conv_b[0]), dtb_col, alog_col)

    t = 256
    x0a, sz, xs, bc, yb, mg, gate, dt = _pass1_call(
        x, mod3, ln0g, ln0b, w_zx, w_uvg, w_dtT, conv_w[0], row(conv_b[0]), dtb_col, alog_col, s_b,
        row(gm_norm_g[0]), row(gm_norm_b[0]), w_spatial[0].astype(bf16), bsx, row(b_gate[0]),
        w_gm_proj[0].astype(bf16), t)
    return _pass2_call(x0a, mod3, sz, xs, bc, yb, mg, gate, dt, alog_col, dsk, row(ssd_norm_g[0]),
                       w_ssd_proj[0].astype(bf16), w_out[0].astype(bf16),
                       row(ln1_g[0]), row(ln1_b[0]), s_f,
                       w_ff1[0].astype(bf16), w_ff3[0].astype(bf16), w_ff2[0].astype(bf16),
                       row(ln2_g[0]), row(ln2_b[0]), 2 * t)
```

```python
import functools

import jax
import jax.numpy as jnp
from jax import lax
from jax.experimental import pallas as pl
from jax.experimental.pallas import tpu as pltpu

f32 = jnp.float32
bf16 = jnp.bfloat16

D = 1024
HEADS = 16
HEAD_DIM = 64
GROUPS = 2
GROUP_W = D // GROUPS
STATE = 128
CONV = 5
CHUNK = 128
XBC = D + 2 * GROUPS * STATE
GM_GROUPS = 8
GM_GROUP_DIM = D // GM_GROUPS
D_FF = 2816
D_MOD = 6 * D
DEPTH = 1
ALPHA = (2 * DEPTH) ** 0.25
LN_EPS = 1e-5
HALO = 8
LANES = 128
NBLK = 256

D_PROJ = D + XBC + 2 * HEADS + 4 * D

VMEM_LIMIT = 56 * 1024 * 1024


def _dot(a, b):
    return jnp.dot(a, b, preferred_element_type=f32)


def _dot_nt(a, b):
    return lax.dot_general(a, b, (((1,), (1,)), ((), ())), preferred_element_type=f32)


def _dot_tn(a, b):
    return lax.dot_general(a, b, (((0,), (0,)), ((), ())), preferred_element_type=f32)


def _norm_rows(xf):
    mu = jnp.mean(xf, axis=-1, keepdims=True)
    xc = xf - mu
    return xc * lax.rsqrt(jnp.mean(xc * xc, axis=-1, keepdims=True) + LN_EPS)


def _ln(xf, g, b):
    return _norm_rows(xf) * g + b


def _sigmoid(x):
    return 0.5 * jnp.tanh(0.5 * x) + 0.5


def _silu(x):
    h = 0.5 * x
    return h * jnp.tanh(h) + h


def _softplus(x):
    return jnp.maximum(x, 0.0) + jnp.log1p(jnp.exp(-jnp.abs(x)))


def _scan_rows(a, tri):
    hi = a.astype(bf16)
    r1 = a - hi.astype(f32)
    mid = r1.astype(bf16)
    lo = (r1 - mid.astype(f32)).astype(bf16)
    cs = _dot(jnp.concatenate([hi, mid, lo], axis=0), tri)
    return cs[0:16] + cs[16:32] + cs[32:48]


def _dir_rows(dt, a_col, fwd):
    k = lax.broadcasted_iota(jnp.int32, (CHUNK, CHUNK), 0)
    j = lax.broadcasted_iota(jnp.int32, (CHUNK, CHUNK), 1)
    tri = jnp.where((k <= j) if fwd else (k >= j), 1.0, 0.0).astype(bf16)
    cum = _scan_rows(dt * a_col, tri)
    total = cum[:, CHUNK - 1:CHUNK] if fwd else cum[:, 0:1]
    return cum, jnp.exp(total - cum) * dt, jnp.exp(cum)


def _to_cols(rows):
    pad = jnp.zeros((LANES - 16 * len(rows), CHUNK), f32)
    return jnp.concatenate(list(rows) + [pad], axis=0).T


def _expand_heads(m):
    lane = lax.broadcasted_iota(jnp.int32, (CHUNK, LANES), 1)
    pieces = []
    for p in range(HEADS // 2):
        pieces.append(jnp.where(lane < HEAD_DIM, m[:, 2 * p:2 * p + 1], m[:, 2 * p + 1:2 * p + 2]))
    return jnp.concatenate(pieces, axis=1)


def _ssd_prep(dt, a_col, fwd):
    cum, wdt, ecum = _dir_rows(dt, a_col, fwd)
    return cum - jnp.log(dt), _to_cols([cum, wdt, ecum])


def _ssd_chunk(prep, xs_bf, xs_f, bm, cm, h_ref, fwd, want_y=True, fillers=()):
    cumj, cols = prep
    fillers = list(fillers)
    ecum_x = _expand_heads(cols[:, 32:48])
    y = None
    if want_y:
        ii = lax.broadcasted_iota(jnp.int32, (CHUNK, CHUNK), 0)
        jj = lax.broadcasted_iota(jnp.int32, (CHUNK, CHUNK), 1)
        keep = (ii >= jj) if fwd else (ii <= jj)
        lane = lax.broadcasted_iota(jnp.int32, (CHUNK, LANES), 1)
        cb = [_dot_nt(cm[:, g * STATE:(g + 1) * STATE], bm[:, g * STATE:(g + 1) * STATE]) for g in range(GROUPS)]
        pieces = []
        for p in range(HEADS // 2):
            g = (2 * p) // (HEADS // GROUPS)
            ws = []
            for h in (2 * p, 2 * p + 1):
                seg = cols[:, h:h + 1] - cumj[h:h + 1, :]
                ws.append(jnp.where(keep, cb[g] * jnp.exp(seg), 0.0))
            lhs = jnp.concatenate(ws, axis=1).astype(bf16)
            x2 = xs_bf[:, p * LANES:(p + 1) * LANES]
            zero = jnp.zeros_like(x2)
            rhs = jnp.concatenate([jnp.where(lane < HEAD_DIM, x2, zero),
                                   jnp.where(lane >= HEAD_DIM, x2, zero)], axis=0)
            pieces.append(_dot(lhs, rhs))
            if fillers:
                fillers.pop(0)()
        y_diag = jnp.concatenate(pieces, axis=1)
        y_off = jnp.concatenate([_dot(cm[:, g * STATE:(g + 1) * STATE], h_ref[g].astype(bf16))
                                 for g in range(GROUPS)], axis=1)
        y = y_diag + y_off * ecum_x
    xw = (xs_f * _expand_heads(cols[:, 16:32])).astype(bf16)
    dec = ecum_x[CHUNK - 1:CHUNK, :] if fwd else ecum_x[0:1, :]
    for g in range(GROUPS):
        st = _dot_tn(bm[:, g * STATE:(g + 1) * STATE], xw[:, g * GROUP_W:(g + 1) * GROUP_W])
        h_ref[g] = dec[:, g * GROUP_W:(g + 1) * GROUP_W] * h_ref[g] + st
    for f in fillers:
        f()
    return y


def _conv_silu(ext_ref, convw_ref, convb_ref, t, c0, c1):
    n = (t + 2 * HALO) // 8
    w = c1 - c0
    p = ext_ref[:, c0:c1].reshape(n, 8, w)
    sub = lax.broadcasted_iota(jnp.int32, (1, 8, w), 1)
    mid = CONV // 2
    acc = convb_ref[:, c0:c1].reshape(1, 1, w) + p[1:n - 1] * convw_ref[mid:mid + 1, c0:c1].reshape(1, 1, w)
    for k in range(CONV):
        s = k - mid
        if s > 0:
            mixed = jnp.where(sub >= s, p[1:n - 1], p[2:n])
            shifted = pltpu.roll(mixed, 8 - s, axis=1)
        elif s < 0:
            mixed = jnp.where(sub < 8 + s, p[1:n - 1], p[0:n - 2])
            shifted = pltpu.roll(mixed, -s, axis=1)
        else:
            continue
        acc = acc + shifted * convw_ref[k:k + 1, c0:c1].reshape(1, 1, w)
    return _silu(acc).reshape(t, w)


def _mod_kernel(c_ref, w_ref, b_ref, o_ref):
    s = _silu(c_ref[...]).astype(bf16)
    o_ref[...] = _dot(s, w_ref[...].astype(bf16)) + b_ref[...]


def _mod_call(cc, w_ada, b_ada):
    nb = 1024
    return pl.pallas_call(
        _mod_kernel,
        grid=(D_MOD // nb,),
        in_specs=[pl.BlockSpec((16, D), lambda n: (0, 0)),
                  pl.BlockSpec((D, nb), lambda n: (0, n)),
                  pl.BlockSpec((1, nb), lambda n: (0, n))],
        out_specs=pl.BlockSpec((16, nb), lambda n: (0, n)),
        out_shape=jax.ShapeDtypeStruct((16, D_MOD), f32),
        compiler_params=pltpu.CompilerParams(dimension_semantics=("arbitrary",), vmem_limit_bytes=VMEM_LIMIT),
        name="adaln_mod",
    )(cc, w_ada, b_ada)


def _ctx_kernel(ctx_ref, mod_ref, ln0g_ref, ln0b_ref, wzx_ref, wdtT_ref, convw_ref, convb_ref, dtb_ref,
                alog_ref, sf_ref, sb_ref, ext_ref, xs_ref, bc_ref):
    cl = ctx_ref.shape[0]
    sh1 = mod_ref[:, 0:D]
    sc1 = mod_ref[:, D:2 * D]
    xm = (_ln(ctx_ref[...], ln0g_ref[...], ln0b_ref[...]) * (1.0 + sc1) + sh1).astype(bf16)
    ext_ref[pl.ds(0, HALO), :] = jnp.zeros((HALO, XBC), f32)
    ext_ref[pl.ds(HALO + cl, HALO), :] = jnp.zeros((HALO, XBC), f32)
    ext_ref[pl.ds(HALO, cl), :] = _dot(xm, wzx_ref[:, D:])
    for c0 in range(0, D, NBLK):
        xs_ref[:, c0:c0 + NBLK] = _conv_silu(ext_ref, convw_ref, convb_ref, cl, c0, c0 + NBLK)
    bc_ref[...] = _conv_silu(ext_ref, convw_ref, convb_ref, cl, D, XBC).astype(bf16)
    dt = _softplus(_dot_nt(wdtT_ref[...], xm) + dtb_ref[...])
    a_col = -jnp.exp(alog_ref[...])
    sf_ref[...] = jnp.zeros(sf_ref.shape, f32)
    sb_ref[...] = jnp.zeros(sb_ref.shape, f32)
    nch = cl // CHUNK
    jobs = [(fwd, h_ref, c, 0 if fwd else HEADS)
            for k in range(nch) for fwd, h_ref, c in ((True, sf_ref, k), (False, sb_ref, nch - 1 - k))]
    preps = [_ssd_prep(dt[r0:r0 + HEADS, c * CHUNK:(c + 1) * CHUNK], a_col[r0:r0 + HEADS], fwd)
             for fwd, _, c, r0 in jobs]
    for prep, (fwd, h_ref, c, _) in zip(preps, jobs):
        rows = slice(c * CHUNK, (c + 1) * CHUNK)
        _ssd_chunk(prep, None, xs_ref[rows, :], bc_ref[rows, 0:GROUPS * STATE], None, h_ref, fwd, want_y=False)


def _ctx_call(ctx, mod3, ln0g, ln0b, w_zx, w_dtT, convw, convb, dtb_col, alog_col):
    bsz, cl, _ = ctx.shape
    full = lambda shape: pl.BlockSpec(shape, lambda b: (0,) * len(shape))
    st_shape = jax.ShapeDtypeStruct((bsz, GROUPS, STATE, GROUP_W), f32)
    st_spec = pl.BlockSpec((None, GROUPS, STATE, GROUP_W), lambda b: (b, 0, 0, 0))
    return pl.pallas_call(
        _ctx_kernel,
        grid=(bsz,),
        in_specs=[pl.BlockSpec((None, cl, D), lambda b: (b, 0, 0)),
                  pl.BlockSpec((None, 1, 2 * D), lambda b: (0, 0, 0)),
                  full((1, D)), full((1, D)), full((D, D + XBC)), full((2 * HEADS, D)),
                  full((CONV, XBC)), full((1, XBC)), full((2 * HEADS, 1)), full((2 * HEADS, 1))],
        out_specs=[st_spec, st_spec],
        out_shape=[st_shape, st_shape],
        scratch_shapes=[pltpu.VMEM((cl + 2 * HALO, XBC), f32), pltpu.VMEM((cl, D), f32),
                        pltpu.VMEM((cl, 2 * GROUPS * STATE), bf16)],
        compiler_params=pltpu.CompilerParams(dimension_semantics=("arbitrary",), vmem_limit_bytes=VMEM_LIMIT),
        name="ctx_states",
    )(ctx, mod3, ln0g, ln0b, w_zx, w_dtT, convw, convb, dtb_col, alog_col)


def _pass1_kernel(x_ref, xp_ref, xn_ref, xq_ref, xqp_ref, xqn_ref, mod_ref, ln0g_ref, ln0b_ref, wzx_ref, wuvg_ref,
                  wdtT_ref, convw_ref, convb_ref, dtb_ref, alog_ref, sb_ref, gmg_ref, gmb_ref, ws_ref, bsx_ref,
                  bg_ref, wgp_ref,
                  x0a_ref, sz_ref, xs_ref, bc_ref, yb_ref, mg_ref, gate_ref, dt_ref,
                  ext_ref, hb_ref, xsf_ref, ug_ref, vg_ref, vn_ref, ygm_ref, gg_ref, xm_ref, xe_ref,
                  xmn_ref, xen_ref, x0an_ref):
    t = x_ref.shape[0]
    nch = t // CHUNK
    step = pl.program_id(1)
    nt = pl.num_programs(1)
    tile = nt - 1 - step

    g0 = ln0g_ref[...]
    b0 = ln0b_ref[...]
    sc1p = 1.0 + mod_ref[:, D:2 * D]
    gm = g0 * sc1p
    bm = b0 * sc1p + mod_ref[:, 0:D]
    parts = 4

    def prepare(xt_ref, xtp_ref, xtn_ref, k):
        r0, r1 = k * t // parts, (k + 1) * t // parts
        xn = _norm_rows(xt_ref[r0:r1, :])
        x0an_ref[r0:r1, :] = xn * (ALPHA * g0) + ALPHA * b0
        xm_f = xn * gm + bm
        xmn_ref[r0:r1, :] = xm_f.astype(bf16)
        if k == 0:
            xen_ref[0:HALO + r1, :] = jnp.concatenate([_norm_rows(xtp_ref[...]) * gm + bm, xm_f],
                                                      axis=0).astype(bf16)
        elif k == parts - 1:
            xen_ref[HALO + r0:, :] = jnp.concatenate([xm_f, _norm_rows(xtn_ref[...]) * gm + bm],
                                                     axis=0).astype(bf16)
        else:
            xen_ref[HALO + r0:HALO + r1, :] = xm_f.astype(bf16)
        bits = pltpu.bitcast(xm_f[0:8, 0:LANES], jnp.uint32)
        return ((bits >> 16) >> 16).astype(f32)[0:1, :]

    @pl.when(step == 0)
    def _():
        hb_ref[...] = sb_ref[...]
        for k in range(parts):
            prepare(x_ref, xp_ref, xn_ref, k)

    xm_ref[...] = xmn_ref[...]
    xe_ref[...] = xen_ref[...]
    x0a_ref[...] = x0an_ref[...]
    row = lax.broadcasted_iota(jnp.int32, (t + 2 * HALO, 1), 0)
    valid = jnp.logical_and(jnp.logical_or(row >= HALO, tile > 0), jnp.logical_or(row < HALO + t, tile < nt - 1))

    def tied(r, tie):
        if tie is None:
            return r
        return r + jnp.concatenate([tie()] * (r.shape[1] // LANES), axis=1)

    def ext_task(c0, tie=None):
        def run():
            w = XBC // 3
            r = tied(_dot(xe_ref[...], wzx_ref[:, D + c0:D + c0 + w]), tie)
            ext_ref[:, c0:c0 + w] = jnp.where(valid, r, 0.0)
        return run

    def proj_task(w_ref, w0, c0, dst_ref, fn, tie=None):
        def run():
            cols = slice(c0, c0 + NBLK)
            r = tied(_dot(xm_ref[...], w_ref[:, w0 + c0:w0 + c0 + NBLK]), tie)
            dst_ref[:, cols] = fn(r, cols).astype(dst_ref.dtype)
        return run

    def conv_task(c0):
        def run():
            blk = _conv_silu(ext_ref, convw_ref, convb_ref, t, c0, c0 + LANES)
            if c0 < D:
                xsf_ref[:, c0:c0 + LANES] = blk
                xs_ref[:, c0:c0 + LANES] = blk.astype(bf16)
            else:
                bc_ref[:, c0 - D:c0 - D + LANES] = blk.astype(bf16)
        return run

    def vnorm_task(c):
        def run():
            rows = slice(c * CHUNK, (c + 1) * CHUNK)
            vn_ref[rows, :] = _ln(vg_ref[rows, :], gmg_ref[...], gmb_ref[...]).astype(bf16)
        return run

    def gating_task(g):
        def run():
            cols = slice(g * GM_GROUP_DIM, (g + 1) * GM_GROUP_DIM)
            rhs = jnp.concatenate([vn_ref[c * CHUNK:(c + 1) * CHUNK, cols] for c in range(nch)], axis=1)
            mixed = _dot(ws_ref[g], rhs)
            for c in range(nch):
                rows = slice(c * CHUNK, (c + 1) * CHUNK)
                ygm_ref[rows, cols] = (ug_ref[rows, cols].astype(f32)
                                       * (mixed[:, c * GM_GROUP_DIM:(c + 1) * GM_GROUP_DIM]
                                          + bsx_ref[:, cols])).astype(bf16)
        return run

    blocks = range(0, D, NBLK)
    gelu = lambda r, cols: jax.nn.gelu(r)
    u_t = [proj_task(wuvg_ref, 0, c0, ug_ref, gelu) for c0 in blocks]
    v_t = [proj_task(wuvg_ref, D, c0, vg_ref, gelu) for c0 in blocks]
    z_t = [proj_task(wzx_ref, 0, c0, sz_ref, lambda r, cols: _silu(r.astype(bf16))) for c0 in blocks]
    gs_t = [proj_task(wuvg_ref, 2 * D, c0, gate_ref, lambda r, cols: _sigmoid((r + bg_ref[:, cols]).astype(bf16)))
            for c0 in blocks]
    gg_t = [proj_task(wuvg_ref, 3 * D, c0, gg_ref,
                      lambda r, cols: _sigmoid(r + bg_ref[:, D + cols.start:D + cols.stop])) for c0 in blocks]

    def gm_task(c0):
        def run():
            cols = slice(c0, c0 + NBLK)
            mg_ref[:, cols] = (gg_ref[:, cols] * _dot(ygm_ref[...], wgp_ref[:, cols])).astype(bf16)
        return run

    gm_t = [gm_task(c0) for c0 in blocks]
    e_t = [ext_task(c0) for c0 in range(0, XBC, XBC // 3)]
    c_t = [conv_task(c0) for c0 in range(0, XBC, LANES)]
    n_t = [vnorm_task(c) for c in range(nch)]
    s_t = [gating_task(g) for g in range(GM_GROUPS)]
    g_t = gg_t + [t_ for pair in zip(gs_t, gm_t) for t_ in pair]
    fill = (len(g_t) - len(z_t)) // nch
    assert (len(v_t), len(c_t), len(s_t)) == (4, 12, 8) and fill * nch == len(g_t) - len(z_t)

    nxt = [functools.partial(prepare, xq_ref, xqp_ref, xqn_ref, k) for k in range(parts)]
    for task in (proj_task(wuvg_ref, D, 0, vg_ref, gelu, nxt[0]), ext_task(0, nxt[1]),
                 proj_task(wuvg_ref, D, NBLK, vg_ref, gelu, nxt[2]), ext_task(XBC // 3, nxt[3]),
                 v_t[2], e_t[2], v_t[3]):
        task()
    dt = _softplus(_dot_nt(wdtT_ref[...], xm_ref[...]) + dtb_ref[...])
    dt_ref[...] = dt
    a_col = -jnp.exp(alog_ref[...])
    preps = [_ssd_prep(dt[HEADS:, c * CHUNK:(c + 1) * CHUNK], a_col[HEADS:], False) for c in range(nch)]
    assert nch == 2
    for task in (u_t[0], c_t[0], u_t[1], c_t[1], n_t[0], u_t[2], c_t[2], n_t[1], u_t[3], c_t[3]):
        task()
    for i in range(4):
        for task in (z_t[i], c_t[4 + 2 * i], g_t[i], c_t[5 + 2 * i], s_t[2 * i], s_t[2 * i + 1]):
            task()
    rest = g_t[len(z_t):]
    for i, c in enumerate(range(nch - 1, -1, -1)):
        rows = slice(c * CHUNK, (c + 1) * CHUNK)
        y = _ssd_chunk(preps[c], xs_ref[rows, :], xsf_ref[rows, :], bc_ref[rows, 0:GROUPS * STATE],
                       bc_ref[rows, GROUPS * STATE:], hb_ref, fwd=False, fillers=rest[i * fill:(i + 1) * fill])
        yb_ref[rows, :] = y.astype(bf16)


def _pass1_call(x, mod3, ln0g, ln0b, w_zx, w_uvg, w_dtT, convw, convb, dtb_col, alog_col, s_b, gmg, gmb, ws, bsx,
                bg, wgp, t):
    bsz, seq, _ = x.shape
    nt = seq // t
    hb = t // HALO
    nhb = seq // HALO
    const = lambda shape: pl.BlockSpec(shape, lambda b, s: (0,) * len(shape), pipeline_mode=pl.Buffered(1))
    tok = lambda w: pl.BlockSpec((None, t, w), lambda b, s: (b, nt - 1 - s, 0))
    act = lambda w, dt_: jax.ShapeDtypeStruct((bsz, seq, w), dt_)
    return pl.pallas_call(
        _pass1_kernel,
        grid=(bsz, nt),
        in_specs=[tok(D),
                  pl.BlockSpec((None, HALO, D), lambda b, s: (b, jnp.maximum((nt - 1 - s) * hb - 1, 0), 0)),
                  pl.BlockSpec((None, HALO, D), lambda b, s: (b, jnp.minimum((nt - s) * hb, nhb - 1), 0)),
                  pl.BlockSpec((None, t, D), lambda b, s: (b, jnp.maximum(nt - 2 - s, 0), 0)),
                  pl.BlockSpec((None, HALO, D), lambda b, s: (b, jnp.maximum((nt - 2 - s) * hb - 1, 0), 0)),
                  pl.BlockSpec((None, HALO, D), lambda b, s: (b, jnp.clip((nt - 1 - s) * hb, 0, nhb - 1), 0)),
                  pl.BlockSpec((None, 1, 2 * D), lambda b, s: (b, 0, 0)),
                  const((1, D)), const((1, D)), const((D, D + XBC)), const((D, 4 * D)), const((2 * HEADS, D)),
                  const((CONV, XBC)), const((1, XBC)), const((2 * HEADS, 1)), const((2 * HEADS, 1)),
                  pl.BlockSpec((None, GROUPS, STATE, GROUP_W), lambda b, s: (b, 0, 0, 0)),
                  const((1, D)), const((1, D)), const((GM_GROUPS, CHUNK, CHUNK)), const((CHUNK, D)),
                  const((1, 2 * D)), const((D, D))],
        out_specs=[tok(D), tok(D), tok(D), tok(2 * GROUPS * STATE), tok(D), tok(D), tok(D),
                   pl.BlockSpec((None, 2 * HEADS, t), lambda b, s: (b, 0, nt - 1 - s))],
        out_shape=[act(D, f32), act(D, bf16), act(D, bf16), act(2 * GROUPS * STATE, bf16), act(D, bf16),
                   act(D, bf16), act(D, bf16), jax.ShapeDtypeStruct((bsz, 2 * HEADS, seq), f32)],
        scratch_shapes=[pltpu.VMEM((t + 2 * HALO, XBC), f32), pltpu.VMEM((GROUPS, STATE, GROUP_W), f32),
                        pltpu.VMEM((t, D), f32), pltpu.VMEM((t, D), bf16), pltpu.VMEM((t, D), f32),
                        pltpu.VMEM((t, D), bf16), pltpu.VMEM((t, D), bf16), pltpu.VMEM((t, D), f32),
                        pltpu.VMEM((t, D), bf16), pltpu.VMEM((t + 2 * HALO, D), bf16),
                        pltpu.VMEM((t, D), bf16), pltpu.VMEM((t + 2 * HALO, D), bf16), pltpu.VMEM((t, D), f32)],
        compiler_params=pltpu.CompilerParams(dimension_semantics=("arbitrary", "arbitrary"),
                                             vmem_limit_bytes=VMEM_LIMIT),
        name="pass1_bwd",
    )(x, x, x, x, x, x, mod3, ln0g, ln0b, w_zx, w_uvg, w_dtT, convw, convb, dtb_col, alog_col, s_b, gmg, gmb, ws, bsx,
      bg, wgp)


def _pass2_kernel(nt, x0a_ref, moda_ref, modb_ref, sz_ref, xs_ref, bc_ref, yb_ref, mg_ref, gate_ref, dt_ref,
                  alog_ref, dsk_ref, ng_ref, wsp_ref, wo_ref, ln1g_ref, ln1b_ref, sf_ref,
                  w1_ref, w3_ref, w2_ref, ln2g_ref, ln2b_ref,
                  o_ref, hf_ref, ys_ref, x1_ref, hm_ref, a_ref, ff_ref):
    t = x0a_ref.shape[0]
    nch = t // CHUNK
    step = pl.program_id(0)

    @pl.when(step == 0)
    def _():
        x1_ref[...] = jnp.zeros(x1_ref.shape, f32)

    @pl.when(step % nt == 0)
    def _():
        hf_ref[...] = sf_ref[...]

    hm_ref[...] = (x1_ref[...] * (1.0 + modb_ref[:, 4 * D:5 * D]) + modb_ref[:, 3 * D:4 * D]).astype(bf16)

    def up_task(n0):
        def run():
            cols = slice(n0, n0 + NBLK)
            a_ref[:, cols] = (_silu(_dot(hm_ref[...], w1_ref[:, cols])) * _dot(hm_ref[...], w3_ref[:, cols])).astype(bf16)
        return run

    def down_task(c0):
        def run():
            ff_ref[:, c0:c0 + NBLK] = _dot(a_ref[...], w2_ref[:, c0:c0 + NBLK])
        return run

    tasks = [up_task(n0) for n0 in range(0, D_FF, NBLK)] + [down_task(c0) for c0 in range(0, D, NBLK)]
    per = -(-len(tasks) // nch)

    a_col = -jnp.exp(alog_ref[...])
    dt = dt_ref[...]
    preps = [_ssd_prep(dt[0:HEADS, c * CHUNK:(c + 1) * CHUNK], a_col[0:HEADS], True) for c in range(nch)]
    dsk = dsk_ref[0:1, :] + dsk_ref[1:2, :]
    for c in range(nch):
        rows = slice(c * CHUNK, (c + 1) * CHUNK)
        xs_bf = xs_ref[rows, :]
        xs_f = xs_bf.astype(f32)
        y = _ssd_chunk(preps[c], xs_bf, xs_f, bc_ref[rows, 0:GROUPS * STATE], bc_ref[rows, GROUPS * STATE:],
                       hf_ref, fwd=True, fillers=tasks[c * per:(c + 1) * per])
        y = y + yb_ref[rows, :].astype(f32) + xs_f * dsk
        hh = y * sz_ref[rows, :].astype(f32)
        hh = hh * lax.rsqrt(jnp.mean(hh * hh, axis=-1, keepdims=True) + LN_EPS) * ng_ref[...]
        ys_ref[rows, :] = hh.astype(bf16)

    o_ref[...] = _ln(ALPHA * x1_ref[...] + modb_ref[:, 5 * D:6 * D] * ff_ref[...], ln2g_ref[...], ln2b_ref[...])

    merged = gate_ref[...].astype(f32) * _dot(ys_ref[...], wsp_ref[...]) + mg_ref[...].astype(f32)
    out_x = _dot(merged.astype(bf16), wo_ref[...])
    x1_ref[...] = _ln(x0a_ref[...] + moda_ref[:, 2 * D:3 * D] * out_x, ln1g_ref[...], ln1b_ref[...])


def _pass2_call(x0a, mod3, sz, xs, bc, yb, mg, gate, dt, alog_col, dsk, ng, wsp, wo, ln1g, ln1b, s_f,
                w1, w3, w2, ln2g, ln2b, t):
    bsz, seq, _ = x0a.shape
    nt = seq // t
    last = bsz * nt - 1
    cur = lambda s: jnp.minimum(s, last)
    prev = lambda s: jnp.maximum(s - 1, 0)
    const = lambda shape: pl.BlockSpec(shape, lambda s: (0,) * len(shape), pipeline_mode=pl.Buffered(1))
    tok = lambda w: pl.BlockSpec((None, t, w), lambda s: (cur(s) // nt, cur(s) % nt, 0))
    return pl.pallas_call(
        functools.partial(_pass2_kernel, nt),
        grid=(bsz * nt + 1,),
        in_specs=[tok(D), pl.BlockSpec((None, 1, D_MOD), lambda s: (cur(s) // nt, 0, 0)),
                  pl.BlockSpec((None, 1, D_MOD), lambda s: (prev(s) // nt, 0, 0)),
                  tok(D), tok(D), tok(2 * GROUPS * STATE), tok(D), tok(D), tok(D),
                  pl.BlockSpec((None, 2 * HEADS, t), lambda s: (cur(s) // nt, 0, cur(s) % nt)),
                  const((2 * HEADS, 1)), const((2, D)), const((1, D)),
                  const((D, D)), const((D, D)), const((1, D)), const((1, D)),
                  pl.BlockSpec((None, GROUPS, STATE, GROUP_W), lambda s: (cur(s) // nt, 0, 0, 0)),
                  const((D, D_FF)), const((D, D_FF)), const((D_FF, D)), const((1, D)), const((1, D))],
        out_specs=pl.BlockSpec((None, t, D), lambda s: (prev(s) // nt, prev(s) % nt, 0)),
        out_shape=jax.ShapeDtypeStruct((bsz, seq, D), f32),
        scratch_shapes=[pltpu.VMEM((GROUPS, STATE, GROUP_W), f32), pltpu.VMEM((t, D), bf16),
                        pltpu.VMEM((t, D), f32), pltpu.VMEM((t, D), bf16),
                        pltpu.VMEM((t, D_FF), bf16), pltpu.VMEM((t, D), f32)],
        compiler_params=pltpu.CompilerParams(dimension_semantics=("arbitrary",), vmem_limit_bytes=VMEM_LIMIT),
        name="pass2_ffn",
    )(x0a, mod3, mod3, sz, xs, bc, yb, mg, gate, dt, alog_col, dsk, ng, wsp, wo, ln1g, ln1b, s_f,
      w1, w3, w2, ln2g, ln2b)


def kernel(x, c, ctx, c_ctx, ln0_g, ln0_b, w_ada, b_ada, w_in, conv_w, conv_b, dt_bias, a_log, d_skip, ssd_norm_g, gm_norm_g, gm_norm_b, w_spatial, b_spatial, b_gate, w_ssd_proj, w_gm_proj, w_out, ln1_g, ln1_b, w_ff1, w_ff3, w_ff2, ln2_g, ln2_b):
    bsz, seq, _ = x.shape
    assert x.shape[2] == D and w_in.shape == (DEPTH, D, D_PROJ) and bsz < 16
    assert seq % 512 == 0 and ctx.shape[1] % CHUNK == 0
    row = lambda a: a.reshape(1, -1)

    w_in0 = w_in[0]
    o_dt = D + XBC
    w_zx = w_in0[:, :o_dt].astype(bf16)
    w_uvg = w_in0[:, o_dt + 2 * HEADS:].astype(bf16)
    w_dtT = w_in0[:, o_dt:o_dt + 2 * HEADS].T.astype(bf16)
    dtb_col = dt_bias[0].reshape(2 * HEADS, 1)
    alog_col = a_log[0].reshape(2 * HEADS, 1)
    dsk = jnp.repeat(d_skip[0], HEAD_DIM, axis=1)
    bsx = jnp.repeat(b_spatial[0].T, GM_GROUP_DIM, axis=1)
    ln0g, ln0b = row(ln0_g), row(ln0_b)

    cc = jnp.concatenate([c, c_ctx[None, :], jnp.zeros((15 - bsz, D), f32)], axis=0)
    mod = _mod_call(cc, w_ada[0], row(b_ada[0]))
    mod3 = mod.reshape(16, 1, D_MOD)
    s_f, s_b = _ctx_call(ctx, mod3[bsz:bsz + 1], ln0g, ln0b, w_zx, w_dtT,
                         conv_w[0], row(conv_b[0]), dtb_col, alog_col)

    t = 256
    x0a, sz, xs, bc, yb, mg, gate, dt = _pass1_call(
        x, mod3, ln0g, ln0b, w_zx, w_uvg, w_dtT, conv_w[0], row(conv_b[0]), dtb_col, alog_col, s_b,
        row(gm_norm_g[0]), row(gm_norm_b[0]), w_spatial[0].astype(bf16), bsx, row(b_gate[0]),
        w_gm_proj[0].astype(bf16), t)
    return _pass2_call(x0a, mod3, sz, xs, bc, yb, mg, gate, dt, alog_col, dsk, row(ssd_norm_g[0]),
                       w_ssd_proj[0].astype(bf16), w_out[0].astype(bf16),
                       row(ln1_g[0]), row(ln1_b[0]), s_f,
                       w_ff1[0].astype(bf16), w_ff3[0].astype(bf16), w_ff2[0].astype(bf16),
                       row(ln2_g[0]), row(ln2_b[0]), 2 * t)
```

```python
import functools

import jax
import jax.numpy as jnp
from jax import lax
from jax.experimental import pallas as pl
from jax.experimental.pallas import tpu as pltpu

f32 = jnp.float32
bf16 = jnp.bfloat16

D = 1024
HEADS = 16
HEAD_DIM = 64
GROUPS = 2
GROUP_W = D // GROUPS
STATE = 128
CONV = 5
CHUNK = 128
XBC = D + 2 * GROUPS * STATE
GM_GROUPS = 8
GM_GROUP_DIM = D // GM_GROUPS
D_FF = 2816
D_MOD = 6 * D
DEPTH = 1
ALPHA = (2 * DEPTH) ** 0.25
LN_EPS = 1e-5
HALO = 8
LANES = 128
NBLK = 256

D_PROJ = D + XBC + 2 * HEADS + 4 * D

VMEM_LIMIT = 56 * 1024 * 1024


def _dot(a, b):
    return jnp.dot(a, b, preferred_element_type=f32)


def _dot_nt(a, b):
    return lax.dot_general(a, b, (((1,), (1,)), ((), ())), preferred_element_type=f32)


def _dot_tn(a, b):
    return lax.dot_general(a, b, (((0,), (0,)), ((), ())), preferred_element_type=f32)


def _norm_rows(xf):
    mu = jnp.mean(xf, axis=-1, keepdims=True)
    xc = xf - mu
    return xc * lax.rsqrt(jnp.mean(xc * xc, axis=-1, keepdims=True) + LN_EPS)


def _ln(xf, g, b):
    return _norm_rows(xf) * g + b


def _sigmoid(x):
    return 0.5 * jnp.tanh(0.5 * x) + 0.5


def _silu(x):
    h = 0.5 * x
    return h * jnp.tanh(h) + h


def _softplus(x):
    return jnp.maximum(x, 0.0) + jnp.log1p(jnp.exp(-jnp.abs(x)))


def _scan_rows(a, tri):
    hi = a.astype(bf16)
    r1 = a - hi.astype(f32)
    mid = r1.astype(bf16)
    lo = (r1 - mid.astype(f32)).astype(bf16)
    cs = _dot(jnp.concatenate([hi, mid, lo], axis=0), tri)
    return cs[0:16] + cs[16:32] + cs[32:48]


def _dir_rows(dt, a_col, fwd):
    k = lax.broadcasted_iota(jnp.int32, (CHUNK, CHUNK), 0)
    j = lax.broadcasted_iota(jnp.int32, (CHUNK, CHUNK), 1)
    tri = jnp.where((k <= j) if fwd else (k >= j), 1.0, 0.0).astype(bf16)
    cum = _scan_rows(dt * a_col, tri)
    total = cum[:, CHUNK - 1:CHUNK] if fwd else cum[:, 0:1]
    return cum, jnp.exp(total - cum) * dt, jnp.exp(cum)


def _to_cols(rows):
    pad = jnp.zeros((LANES - 16 * len(rows), CHUNK), f32)
    return jnp.concatenate(list(rows) + [pad], axis=0).T


def _expand_heads(m):
    lane = lax.broadcasted_iota(jnp.int32, (CHUNK, LANES), 1)
    pieces = []
    for p in range(HEADS // 2):
        pieces.append(jnp.where(lane < HEAD_DIM, m[:, 2 * p:2 * p + 1], m[:, 2 * p + 1:2 * p + 2]))
    return jnp.concatenate(pieces, axis=1)


def _ssd_prep(dt, a_col, fwd):
    cum, wdt, ecum = _dir_rows(dt, a_col, fwd)
    return cum - jnp.log(dt), _to_cols([cum, wdt, ecum])


def _ssd_chunk(prep, xs_bf, xs_f, bm, cm, h_ref, fwd, want_y=True, fillers=()):
    cumj, cols = prep
    fillers = list(fillers)
    ecum_x = _expand_heads(cols[:, 32:48])
    y = None
    if want_y:
        ii = lax.broadcasted_iota(jnp.int32, (CHUNK, CHUNK), 0)
        jj = lax.broadcasted_iota(jnp.int32, (CHUNK, CHUNK), 1)
        keep = (ii >= jj) if fwd else (ii <= jj)
        lane = lax.broadcasted_iota(jnp.int32, (CHUNK, LANES), 1)
        cb = [_dot_nt(cm[:, g * STATE:(g + 1) * STATE], bm[:, g * STATE:(g + 1) * STATE]) for g in range(GROUPS)]
        pieces = []
        for p in range(HEADS // 2):
            g = (2 * p) // (HEADS // GROUPS)
            ws = []
            for h in (2 * p, 2 * p + 1):
                seg = cols[:, h:h + 1] - cumj[h:h + 1, :]
                ws.append(jnp.where(keep, cb[g] * jnp.exp(seg), 0.0))
            lhs = jnp.concatenate(ws, axis=1).astype(bf16)
            x2 = xs_bf[:, p * LANES:(p + 1) * LANES]
            zero = jnp.zeros_like(x2)
            rhs = jnp.concatenate([jnp.where(lane < HEAD_DIM, x2, zero),
                                   jnp.where(lane >= HEAD_DIM, x2, zero)], axis=0)
            pieces.append(_dot(lhs, rhs))
            if fillers:
                fillers.pop(0)()
        y_diag = jnp.concatenate(pieces, axis=1)
        y_off = jnp.concatenate([_dot(cm[:, g * STATE:(g + 1) * STATE], h_ref[g].astype(bf16))
                                 for g in range(GROUPS)], axis=1)
        y = y_diag + y_off * ecum_x
    xw = (xs_f * _expand_heads(cols[:, 16:32])).astype(bf16)
    dec = ecum_x[CHUNK - 1:CHUNK, :] if fwd else ecum_x[0:1, :]
    for g in range(GROUPS):
        st = _dot_tn(bm[:, g * STATE:(g + 1) * STATE], xw[:, g * GROUP_W:(g + 1) * GROUP_W])
        h_ref[g] = dec[:, g * GROUP_W:(g + 1) * GROUP_W] * h_ref[g] + st
    for f in fillers:
        f()
    return y


def _conv_silu(ext_ref, convw_ref, convb_ref, t, c0, c1):
    n = (t + 2 * HALO) // 8
    w = c1 - c0
    p = ext_ref[:, c0:c1].reshape(n, 8, w)
    sub = lax.broadcasted_iota(jnp.int32, (1, 8, w), 1)
    mid = CONV // 2
    acc = convb_ref[:, c0:c1].reshape(1, 1, w) + p[1:n - 1] * convw_ref[mid:mid + 1, c0:c1].reshape(1, 1, w)
    for k in range(CONV):
        s = k - mid
        if s > 0:
            mixed = jnp.where(sub >= s, p[1:n - 1], p[2:n])
            shifted = pltpu.roll(mixed, 8 - s, axis=1)
        elif s < 0:
            mixed = jnp.where(sub < 8 + s, p[1:n - 1], p[0:n - 2])
            shifted = pltpu.roll(mixed, -s, axis=1)
        else:
            continue
        acc = acc + shifted * convw_ref[k:k + 1, c0:c1].reshape(1, 1, w)
    return _silu(acc).reshape(t, w)


def _mod_kernel(c_ref, w_ref, b_ref, o_ref):
    s = _silu(c_ref[...]).astype(bf16)
    o_ref[...] = _dot(s, w_ref[...].astype(bf16)) + b_ref[...]


def _mod_call(cc, w_ada, b_ada):
    nb = 1024
    return pl.pallas_call(
        _mod_kernel,
        grid=(D_MOD // nb,),
        in_specs=[pl.BlockSpec((16, D), lambda n: (0, 0)),
                  pl.BlockSpec((D, nb), lambda n: (0, n)),
                  pl.BlockSpec((1, nb), lambda n: (0, n))],
        out_specs=pl.BlockSpec((16, nb), lambda n: (0, n)),
        out_shape=jax.ShapeDtypeStruct((16, D_MOD), f32),
        compiler_params=pltpu.CompilerParams(dimension_semantics=("arbitrary",), vmem_limit_bytes=VMEM_LIMIT),
        name="adaln_mod",
    )(cc, w_ada, b_ada)


def _ctx_kernel(ctx_ref, mod_ref, ln0g_ref, ln0b_ref, wzx_ref, wdtT_ref, convw_ref, convb_ref, dtb_ref,
                alog_ref, sf_ref, sb_ref, ext_ref, xs_ref, bc_ref):
    nb, cl, _ = ctx_ref.shape
    sh1 = mod_ref[:, 0:D]
    sc1 = mod_ref[:, D:2 * D]
    xm = (_ln(ctx_ref[...].reshape(nb * cl, D), ln0g_ref[...], ln0b_ref[...]) * (1.0 + sc1) + sh1).astype(bf16)
    xbc = _dot(xm, wzx_ref[:, D:])
    dt = _softplus(_dot_nt(wdtT_ref[...], xm) + dtb_ref[...])
    a_col = -jnp.exp(alog_ref[...])
    sf_ref[...] = jnp.zeros(sf_ref.shape, f32)
    sb_ref[...] = jnp.zeros(sb_ref.shape, f32)
    for b in range(nb):
        ext = ext_ref.at[b]
        ext[pl.ds(0, HALO), :] = jnp.zeros((HALO, XBC), f32)
        ext[pl.ds(HALO + cl, HALO), :] = jnp.zeros((HALO, XBC), f32)
        ext[pl.ds(HALO, cl), :] = xbc[b * cl:(b + 1) * cl, :]
        for c0 in range(0, D, NBLK):
            xs_ref[b, :, c0:c0 + NBLK] = _conv_silu(ext, convw_ref, convb_ref, cl, c0, c0 + NBLK)
        bc_ref[b] = _conv_silu(ext, convw_ref, convb_ref, cl, D, XBC).astype(bf16)
    nch = cl // CHUNK
    jobs = [(b, fwd, h_ref, c, 0 if fwd else HEADS)
            for k in range(nch) for b in range(nb)
            for fwd, h_ref, c in ((True, sf_ref, k), (False, sb_ref, nch - 1 - k))]
    preps = [_ssd_prep(dt[r0:r0 + HEADS, b * cl + c * CHUNK:b * cl + (c + 1) * CHUNK], a_col[r0:r0 + HEADS], fwd)
             for b, fwd, _, c, r0 in jobs]
    for prep, (b, fwd, h_ref, c, _) in zip(preps, jobs):
        rows = slice(c * CHUNK, (c + 1) * CHUNK)
        _ssd_chunk(prep, None, xs_ref[b, rows, :], bc_ref[b, rows, 0:GROUPS * STATE], None, h_ref.at[b], fwd,
                   want_y=False)


def _ctx_call(ctx, mod3, ln0g, ln0b, w_zx, w_dtT, convw, convb, dtb_col, alog_col):
    bsz, cl, _ = ctx.shape
    nb = 4 if bsz % 4 == 0 else 1
    full = lambda shape: pl.BlockSpec(shape, lambda b: (0,) * len(shape))
    st_shape = jax.ShapeDtypeStruct((bsz, GROUPS, STATE, GROUP_W), f32)
    st_spec = pl.BlockSpec((nb, GROUPS, STATE, GROUP_W), lambda b: (b, 0, 0, 0))
    return pl.pallas_call(
        _ctx_kernel,
        grid=(bsz // nb,),
        in_specs=[pl.BlockSpec((nb, cl, D), lambda b: (b, 0, 0)),
                  pl.BlockSpec((None, 1, 2 * D), lambda b: (0, 0, 0)),
                  full((1, D)), full((1, D)), full((D, D + XBC)), full((2 * HEADS, D)),
                  full((CONV, XBC)), full((1, XBC)), full((2 * HEADS, 1)), full((2 * HEADS, 1))],
        out_specs=[st_spec, st_spec],
        out_shape=[st_shape, st_shape],
        scratch_shapes=[pltpu.VMEM((nb, cl + 2 * HALO, XBC), f32), pltpu.VMEM((nb, cl, D), f32),
                        pltpu.VMEM((nb, cl, 2 * GROUPS * STATE), bf16)],
        compiler_params=pltpu.CompilerParams(dimension_semantics=("arbitrary",), vmem_limit_bytes=VMEM_LIMIT),
        name="ctx_states",
    )(ctx, mod3, ln0g, ln0b, w_zx, w_dtT, convw, convb, dtb_col, alog_col)


def _pass1_kernel(x_ref, xp_ref, xn_ref, mod_ref, ln0g_ref, ln0b_ref, wzx_ref, wuvg_ref, wdtT_ref, convw_ref,
                  convb_ref, dtb_ref, alog_ref, sb_ref, gmg_ref, gmb_ref, ws_ref, bsx_ref, bg_ref, wgp_ref,
                  x0a_ref, sz_ref, xs_ref, bc_ref, yb_ref, mg_ref, gate_ref, dt_ref,
                  ext_ref, hb_ref, xsf_ref, ug_ref, vg_ref, vn_ref, ygm_ref, gg_ref):
    t = x_ref.shape[0]
    nch = t // CHUNK
    step = pl.program_id(1)
    nt = pl.num_programs(1)
    tile = nt - 1 - step

    @pl.when(step == 0)
    def _():
        hb_ref[...] = sb_ref[...]

    g0 = ln0g_ref[...]
    b0 = ln0b_ref[...]
    sc1p = 1.0 + mod_ref[:, D:2 * D]
    gm = g0 * sc1p
    bm = b0 * sc1p + mod_ref[:, 0:D]

    xn = _norm_rows(x_ref[...])
    x0a_ref[...] = xn * (ALPHA * g0) + ALPHA * b0
    xm_f = xn * gm + bm
    xm = xm_f.astype(bf16)
    ext = jnp.concatenate([_norm_rows(xp_ref[...]) * gm + bm, xm_f, _norm_rows(xn_ref[...]) * gm + bm],
                          axis=0).astype(bf16)
    row = lax.broadcasted_iota(jnp.int32, (t + 2 * HALO, 1), 0)
    valid = jnp.logical_and(jnp.logical_or(row >= HALO, tile > 0), jnp.logical_or(row < HALO + t, tile < nt - 1))

    def ext_task(c0):
        def run():
            w = XBC // 3
            ext_ref[:, c0:c0 + w] = jnp.where(valid, _dot(ext, wzx_ref[:, D + c0:D + c0 + w]), 0.0)
        return run

    def proj_task(w_ref, w0, c0, dst_ref, fn):
        def run():
            cols = slice(c0, c0 + NBLK)
            dst_ref[:, cols] = fn(_dot(xm, w_ref[:, w0 + c0:w0 + c0 + NBLK]), cols).astype(dst_ref.dtype)
        return run

    def conv_task(c0):
        def run():
            blk = _conv_silu(ext_ref, convw_ref, convb_ref, t, c0, c0 + LANES)
            if c0 < D:
                xsf_ref[:, c0:c0 + LANES] = blk
                xs_ref[:, c0:c0 + LANES] = blk.astype(bf16)
            else:
                bc_ref[:, c0 - D:c0 - D + LANES] = blk.astype(bf16)
        return run

    def vnorm_task(c):
        def run():
            rows = slice(c * CHUNK, (c + 1) * CHUNK)
            vn_ref[rows, :] = _ln(vg_ref[rows, :], gmg_ref[...], gmb_ref[...]).astype(bf16)
        return run

    def gating_task(g):
        def run():
            cols = slice(g * GM_GROUP_DIM, (g + 1) * GM_GROUP_DIM)
            rhs = jnp.concatenate([vn_ref[c * CHUNK:(c + 1) * CHUNK, cols] for c in range(nch)], axis=1)
            mixed = _dot(ws_ref[g], rhs)
            for c in range(nch):
                rows = slice(c * CHUNK, (c + 1) * CHUNK)
                ygm_ref[rows, cols] = (ug_ref[rows, cols].astype(f32)
                                       * (mixed[:, c * GM_GROUP_DIM:(c + 1) * GM_GROUP_DIM]
                                          + bsx_ref[:, cols])).astype(bf16)
        return run

    blocks = range(0, D, NBLK)
    gelu = lambda r, cols: jax.nn.gelu(r)
    u_t = [proj_task(wuvg_ref, 0, c0, ug_ref, gelu) for c0 in blocks]
    v_t = [proj_task(wuvg_ref, D, c0, vg_ref, gelu) for c0 in blocks]
    z_t = [proj_task(wzx_ref, 0, c0, sz_ref, lambda r, cols: _silu(r.astype(bf16))) for c0 in blocks]
    gs_t = [proj_task(wuvg_ref, 2 * D, c0, gate_ref, lambda r, cols: _sigmoid((r + bg_ref[:, cols]).astype(bf16)))
            for c0 in blocks]
    gg_t = [proj_task(wuvg_ref, 3 * D, c0, gg_ref,
                      lambda r, cols: _sigmoid(r + bg_ref[:, D + cols.start:D + cols.stop])) for c0 in blocks]

    def gm_task(c0):
        def run():
            cols = slice(c0, c0 + NBLK)
            mg_ref[:, cols] = (gg_ref[:, cols] * _dot(ygm_ref[...], wgp_ref[:, cols])).astype(bf16)
        return run

    gm_t = [gm_task(c0) for c0 in blocks]
    e_t = [ext_task(c0) for c0 in range(0, XBC, XBC // 3)]
    c_t = [conv_task(c0) for c0 in range(0, XBC, LANES)]
    n_t = [vnorm_task(c) for c in range(nch)]
    s_t = [gating_task(g) for g in range(GM_GROUPS)]
    g_t = gg_t + [t_ for pair in zip(gs_t, gm_t) for t_ in pair]
    fill = (len(g_t) - len(z_t)) // nch
    assert (len(v_t), len(c_t), len(s_t)) == (4, 12, 8) and fill * nch == len(g_t) - len(z_t)

    for task in (v_t[0], e_t[0], v_t[1], e_t[1], v_t[2], e_t[2], v_t[3]):
        task()
    dt = _softplus(_dot_nt(wdtT_ref[...], xm) + dtb_ref[...])
    dt_ref[...] = dt
    a_col = -jnp.exp(alog_ref[...])
    preps = [_ssd_prep(dt[HEADS:, c * CHUNK:(c + 1) * CHUNK], a_col[HEADS:], False) for c in range(nch)]
    assert nch == 2
    for task in (u_t[0], c_t[0], u_t[1], c_t[1], n_t[0], u_t[2], c_t[2], n_t[1], u_t[3], c_t[3]):
        task()
    for i in range(4):
        for task in (z_t[i], c_t[4 + 2 * i], g_t[i], c_t[5 + 2 * i], s_t[2 * i], s_t[2 * i + 1]):
            task()
    rest = g_t[len(z_t):]
    for i, c in enumerate(range(nch - 1, -1, -1)):
        rows = slice(c * CHUNK, (c + 1) * CHUNK)
        y = _ssd_chunk(preps[c], xs_ref[rows, :], xsf_ref[rows, :], bc_ref[rows, 0:GROUPS * STATE],
                       bc_ref[rows, GROUPS * STATE:], hb_ref, fwd=False, fillers=rest[i * fill:(i + 1) * fill])
        yb_ref[rows, :] = y.astype(bf16)


def _pass1_call(x, mod3, ln0g, ln0b, w_zx, w_uvg, w_dtT, convw, convb, dtb_col, alog_col, s_b, gmg, gmb, ws, bsx,
                bg, wgp, t):
    bsz, seq, _ = x.shape
    nt = seq // t
    hb = t // HALO
    nhb = seq // HALO
    const = lambda shape: pl.BlockSpec(shape, lambda b, s: (0,) * len(shape), pipeline_mode=pl.Buffered(1))
    tok = lambda w: pl.BlockSpec((None, t, w), lambda b, s: (b, nt - 1 - s, 0))
    act = lambda w, dt_: jax.ShapeDtypeStruct((bsz, seq, w), dt_)
    return pl.pallas_call(
        _pass1_kernel,
        grid=(bsz, nt),
        in_specs=[tok(D),
                  pl.BlockSpec((None, HALO, D), lambda b, s: (b, jnp.maximum((nt - 1 - s) * hb - 1, 0), 0)),
                  pl.BlockSpec((None, HALO, D), lambda b, s: (b, jnp.minimum((nt - s) * hb, nhb - 1), 0)),
                  pl.BlockSpec((None, 1, 2 * D), lambda b, s: (b, 0, 0)),
                  const((1, D)), const((1, D)), const((D, D + XBC)), const((D, 4 * D)), const((2 * HEADS, D)),
                  const((CONV, XBC)), const((1, XBC)), const((2 * HEADS, 1)), const((2 * HEADS, 1)),
                  pl.BlockSpec((None, GROUPS, STATE, GROUP_W), lambda b, s: (b, 0, 0, 0)),
                  const((1, D)), const((1, D)), const((GM_GROUPS, CHUNK, CHUNK)), const((CHUNK, D)),
                  const((1, 2 * D)), const((D, D))],
        out_specs=[tok(D), tok(D), tok(D), tok(2 * GROUPS * STATE), tok(D), tok(D), tok(D),
                   pl.BlockSpec((None, 2 * HEADS, t), lambda b, s: (b, 0, nt - 1 - s))],
        out_shape=[act(D, f32), act(D, bf16), act(D, bf16), act(2 * GROUPS * STATE, bf16), act(D, bf16),
                   act(D, bf16), act(D, bf16), jax.ShapeDtypeStruct((bsz, 2 * HEADS, seq), f32)],
        scratch_shapes=[pltpu.VMEM((t + 2 * HALO, XBC), f32), pltpu.VMEM((GROUPS, STATE, GROUP_W), f32),
                        pltpu.VMEM((t, D), f32), pltpu.VMEM((t, D), bf16), pltpu.VMEM((t, D), f32),
                        pltpu.VMEM((t, D), bf16), pltpu.VMEM((t, D), bf16), pltpu.VMEM((t, D), f32)],
        compiler_params=pltpu.CompilerParams(dimension_semantics=("arbitrary", "arbitrary"),
                                             vmem_limit_bytes=VMEM_LIMIT),
        name="pass1_bwd",
    )(x, x, x, mod3, ln0g, ln0b, w_zx, w_uvg, w_dtT, convw, convb, dtb_col, alog_col, s_b, gmg, gmb, ws, bsx, bg,
      wgp)


def _pass2_kernel(nt, x0a_ref, moda_ref, modb_ref, sz_ref, xs_ref, bc_ref, yb_ref, mg_ref, gate_ref, dt_ref,
                  alog_ref, dsk_ref, ng_ref, wsp_ref, wo_ref, ln1g_ref, ln1b_ref, sf_ref,
                  w1_ref, w3_ref, w2_ref, ln2g_ref, ln2b_ref,
                  o_ref, hf_ref, ys_ref, x1_ref, hm_ref, a_ref, ff_ref):
    t = x0a_ref.shape[0]
    nch = t // CHUNK
    step = pl.program_id(0)

    @pl.when(step == 0)
    def _():
        x1_ref[...] = jnp.zeros(x1_ref.shape, f32)

    @pl.when(step % nt == 0)
    def _():
        hf_ref[...] = sf_ref[...]

    hm_ref[...] = (x1_ref[...] * (1.0 + modb_ref[:, 4 * D:5 * D]) + modb_ref[:, 3 * D:4 * D]).astype(bf16)

    def up_task(n0):
        def run():
            cols = slice(n0, n0 + NBLK)
            a_ref[:, cols] = (_silu(_dot(hm_ref[...], w1_ref[:, cols])) * _dot(hm_ref[...], w3_ref[:, cols])).astype(bf16)
        return run

    def down_task(c0):
        def run():
            ff_ref[:, c0:c0 + NBLK] = _dot(a_ref[...], w2_ref[:, c0:c0 + NBLK])
        return run

    tasks = [up_task(n0) for n0 in range(0, D_FF, NBLK)] + [down_task(c0) for c0 in range(0, D, NBLK)]
    per = -(-len(tasks) // nch)

    a_col = -jnp.exp(alog_ref[...])
    dt = dt_ref[...]
    preps = [_ssd_prep(dt[0:HEADS, c * CHUNK:(c + 1) * CHUNK], a_col[0:HEADS], True) for c in range(nch)]
    dsk = dsk_ref[0:1, :] + dsk_ref[1:2, :]
    for c in range(nch):
        rows = slice(c * CHUNK, (c + 1) * CHUNK)
        xs_bf = xs_ref[rows, :]
        xs_f = xs_bf.astype(f32)
        y = _ssd_chunk(preps[c], xs_bf, xs_f, bc_ref[rows, 0:GROUPS * STATE], bc_ref[rows, GROUPS * STATE:],
                       hf_ref, fwd=True, fillers=tasks[c * per:(c + 1) * per])
        y = y + yb_ref[rows, :].astype(f32) + xs_f * dsk
        hh = y * sz_ref[rows, :].astype(f32)
        hh = hh * lax.rsqrt(jnp.mean(hh * hh, axis=-1, keepdims=True) + LN_EPS) * ng_ref[...]
        ys_ref[rows, :] = hh.astype(bf16)

    o_ref[...] = _ln(ALPHA * x1_ref[...] + modb_ref[:, 5 * D:6 * D] * ff_ref[...], ln2g_ref[...], ln2b_ref[...])

    merged = gate_ref[...].astype(f32) * _dot(ys_ref[...], wsp_ref[...]) + mg_ref[...].astype(f32)
    out_x = _dot(merged.astype(bf16), wo_ref[...])
    x1_ref[...] = _ln(x0a_ref[...] + moda_ref[:, 2 * D:3 * D] * out_x, ln1g_ref[...], ln1b_ref[...])


def _pass2_call(x0a, mod3, sz, xs, bc, yb, mg, gate, dt, alog_col, dsk, ng, wsp, wo, ln1g, ln1b, s_f,
                w1, w3, w2, ln2g, ln2b, t):
    bsz, seq, _ = x0a.shape
    nt = seq // t
    last = bsz * nt - 1
    cur = lambda s: jnp.minimum(s, last)
    prev = lambda s: jnp.maximum(s - 1, 0)
    const = lambda shape: pl.BlockSpec(shape, lambda s: (0,) * len(shape), pipeline_mode=pl.Buffered(1))
    tok = lambda w: pl.BlockSpec((None, t, w), lambda s: (cur(s) // nt, cur(s) % nt, 0))
    return pl.pallas_call(
        functools.partial(_pass2_kernel, nt),
        grid=(bsz * nt + 1,),
        in_specs=[tok(D), pl.BlockSpec((None, 1, D_MOD), lambda s: (cur(s) // nt, 0, 0)),
                  pl.BlockSpec((None, 1, D_MOD), lambda s: (prev(s) // nt, 0, 0)),
                  tok(D), tok(D), tok(2 * GROUPS * STATE), tok(D), tok(D), tok(D),
                  pl.BlockSpec((None, 2 * HEADS, t), lambda s: (cur(s) // nt, 0, cur(s) % nt)),
                  const((2 * HEADS, 1)), const((2, D)), const((1, D)),
                  const((D, D)), const((D, D)), const((1, D)), const((1, D)),
                  pl.BlockSpec((None, GROUPS, STATE, GROUP_W), lambda s: (cur(s) // nt, 0, 0, 0)),
                  const((D, D_FF)), const((D, D_FF)), const((D_FF, D)), const((1, D)), const((1, D))],
        out_specs=pl.BlockSpec((None, t, D), lambda s: (prev(s) // nt, prev(s) % nt, 0)),
        out_shape=jax.ShapeDtypeStruct((bsz, seq, D), f32),
        scratch_shapes=[pltpu.VMEM((GROUPS, STATE, GROUP_W), f32), pltpu.VMEM((t, D), bf16),
                        pltpu.VMEM((t, D), f32), pltpu.VMEM((t, D), bf16),
                        pltpu.VMEM((t, D_FF), bf16), pltpu.VMEM((t, D), f32)],
        compiler_params=pltpu.CompilerParams(dimension_semantics=("arbitrary",), vmem_limit_bytes=VMEM_LIMIT),
        name="pass2_ffn",
    )(x0a, mod3, mod3, sz, xs, bc, yb, mg, gate, dt, alog_col, dsk, ng, wsp, wo, ln1g, ln1b, s_f,
      w1, w3, w2, ln2g, ln2b)


def kernel(x, c, ctx, c_ctx, ln0_g, ln0_b, w_ada, b_ada, w_in, conv_w, conv_b, dt_bias, a_log, d_skip, ssd_norm_g, gm_norm_g, gm_norm_b, w_spatial, b_spatial, b_gate, w_ssd_proj, w_gm_proj, w_out, ln1_g, ln1_b, w_ff1, w_ff3, w_ff2, ln2_g, ln2_b):
    bsz, seq, _ = x.shape
    assert x.shape[2] == D and w_in.shape == (DEPTH, D, D_PROJ) and bsz < 16
    assert seq % 512 == 0 and ctx.shape[1] % CHUNK == 0
    row = lambda a: a.reshape(1, -1)

    w_in0 = w_in[0]
    o_dt = D + XBC
    w_zx = w_in0[:, :o_dt].astype(bf16)
    w_uvg = w_in0[:, o_dt + 2 * HEADS:].astype(bf16)
    w_dtT = w_in0[:, o_dt:o_dt + 2 * HEADS].T.astype(bf16)
    dtb_col = dt_bias[0].reshape(2 * HEADS, 1)
    alog_col = a_log[0].reshape(2 * HEADS, 1)
    dsk = jnp.repeat(d_skip[0], HEAD_DIM, axis=1)
    bsx = jnp.repeat(b_spatial[0].T, GM_GROUP_DIM, axis=1)
    ln0g, ln0b = row(ln0_g), row(ln0_b)

    cc = jnp.concatenate([c, c_ctx[None, :], jnp.zeros((15 - bsz, D), f32)], axis=0)
    mod = _mod_call(cc, w_ada[0], row(b_ada[0]))
    mod3 = mod.reshape(16, 1, D_MOD)
    s_f, s_b = _ctx_call(ctx, mod3[bsz:bsz + 1], ln0g, ln0b, w_zx, w_dtT,
                         conv_w[0], row(conv_b[0]), dtb_col, alog_col)

    t = 256
    x0a, sz, xs, bc, yb, mg, gate, dt = _pass1_call(
        x, mod3, ln0g, ln0b, w_zx, w_uvg, w_dtT, conv_w[0], row(conv_b[0]), dtb_col, alog_col, s_b,
        row(gm_norm_g[0]), row(gm_norm_b[0]), w_spatial[0].astype(bf16), bsx, row(b_gate[0]),
        w_gm_proj[0].astype(bf16), t)
    return _pass2_call(x0a, mod3, sz, xs, bc, yb, mg, gate, dt, alog_col, dsk, row(ssd_norm_g[0]),
                       w_ssd_proj[0].astype(bf16), w_out[0].astype(bf16),
                       row(ln1_g[0]), row(ln1_b[0]), s_f,
                       w_ff1[0].astype(bf16), w_ff3[0].astype(bf16), w_ff2[0].astype(bf16),
                       row(ln2_g[0]), row(ln2_b[0]), 2 * t)
```
